```python
import math
import jax, jax.numpy as jnp
from jax import lax
import numpy as np

D_MODEL = 2048
BATCH = 2
SEQ = 16384
DEPTH = 1
DEC_BATCH = 4
DEC_SEQ = 4096
PAST_LEN = 128

N_Q_HEADS = 16
N_KV_HEADS = 4
HEAD_DIM = D_MODEL // N_Q_HEADS
Q_GROUP = N_Q_HEADS // N_KV_HEADS
D_ATTN = N_Q_HEADS * HEAD_DIM
D_KV = N_KV_HEADS * HEAD_DIM
WINDOW = 128
BLOCK = 128
NB_SIDE = -(-WINDOW // BLOCK)
KEY_BLOCK = (2 * NB_SIDE + 1) * BLOCK
D_SSM = D_MODEL // 2
SSM_GROUP = 16
N_SSM_GROUPS = D_SSM // SSM_GROUP
SSM_STATE = 64
N_DIR = 2
N_EXPERTS = 16
EC_CAPACITY_FACTOR = 2
D_EXPERT = D_MODEL
D_IN = D_ATTN + 2 * D_KV + D_SSM + 2 * D_MODEL
SPLITS = [D_ATTN, D_ATTN + D_KV, D_ATTN + 2 * D_KV, D_ATTN + 2 * D_KV + D_SSM,
          D_ATTN + 2 * D_KV + D_SSM + D_MODEL]
EPS = 1e-6
NEG_INF = -1e30

kernel_name = "hybrid_bidir_swa_s5_expert_choice_encoder"


def rms_norm(x, g):
    xf = x.astype(jnp.float32)
    y = xf * lax.rsqrt(jnp.mean(xf * xf, axis=-1, keepdims=True) + EPS) * g.astype(jnp.float32)
    return y.astype(x.dtype)


def alibi_slopes():
    h = jnp.arange(1, N_Q_HEADS + 1, dtype=jnp.float32)
    return jnp.exp2(-8.0 * h / N_Q_HEADS)


def windowed_gqa(q, k, v, sink, g_q, g_k):
    b, s = q.shape[0], q.shape[1]
    nb = s // BLOCK
    qf = rms_norm(q, g_q).astype(jnp.float32) * (HEAD_DIM ** -0.5)
    kf = rms_norm(k, g_k).astype(jnp.float32)
    vf = v.astype(jnp.float32)
    pad = ((0, 0), (NB_SIDE * BLOCK, NB_SIDE * BLOCK), (0, 0), (0, 0))
    kp = jnp.pad(kf, pad).reshape(b, nb + 2 * NB_SIDE, BLOCK, N_KV_HEADS, HEAD_DIM)
    vp = jnp.pad(vf, pad).reshape(b, nb + 2 * NB_SIDE, BLOCK, N_KV_HEADS, HEAD_DIM)
    kb = jnp.concatenate([kp[:, o:o + nb] for o in range(2 * NB_SIDE + 1)], axis=2)
    vb = jnp.concatenate([vp[:, o:o + nb] for o in range(2 * NB_SIDE + 1)], axis=2)
    qb = qf.reshape(b, nb, BLOCK, N_KV_HEADS, Q_GROUP, HEAD_DIM)
    scores = jnp.einsum('bnikgd,bnjkd->bnkgij', qb, kb)
    i = jnp.arange(BLOCK)[:, None]
    j = jnp.arange(KEY_BLOCK)[None, :]
    dist = jnp.abs(i - j + NB_SIDE * BLOCK)
    kpos = (jnp.arange(nb)[:, None] - NB_SIDE) * BLOCK + jnp.arange(KEY_BLOCK)[None, :]
    valid = (dist <= WINDOW)[None] & ((kpos >= 0) & (kpos < s))[:, None, :]
    slopes = alibi_slopes().reshape(N_KV_HEADS, Q_GROUP)
    scores = scores - slopes[None, None, :, :, None, None] * dist.astype(jnp.float32)
    scores = jnp.where(valid[None, :, None, None], scores, NEG_INF)
    sink_b = sink.astype(jnp.float32).reshape(N_KV_HEADS, Q_GROUP)[None, None, :, :, None, None]
    m = jnp.maximum(jnp.max(scores, axis=-1, keepdims=True), sink_b)
    p = jnp.exp(scores - m)
    p = p / (jnp.sum(p, axis=-1, keepdims=True) + jnp.exp(sink_b - m))
    out = jnp.einsum('bnkgij,bnjkd->bnikgd', p, vb)
    return out.reshape(b, s, D_ATTN).astype(q.dtype)


def _linear_recurrence(e1, e2):
    a1, b1 = e1
    a2, b2 = e2
    return a1 * a2, a2 * b1 + b2


def bidir_s5(u, lam_re, lam_im, log_step, b_re, b_im, c_re, c_im, d_skip):
    bsz, s = u.shape[0], u.shape[1]
    uf = u.astype(jnp.float32).reshape(bsz, s, N_SSM_GROUPS, SSM_GROUP)
    uc = uf.astype(jnp.complex64)
    y = d_skip.astype(jnp.float32).reshape(N_SSM_GROUPS, SSM_GROUP) * uf
    for direction in range(N_DIR):
        lam = lax.complex(lam_re[direction].astype(jnp.float32), lam_im[direction].astype(jnp.float32))
        step = jnp.exp(log_step[direction].astype(jnp.float32))[:, None]
        lam_bar = jnp.exp(lam * step)
        bmat = lax.complex(b_re[direction].astype(jnp.float32), b_im[direction].astype(jnp.float32))
        b_bar = ((lam_bar - 1.0) / lam)[..., None] * bmat
        bu = jnp.einsum('bsgh,gph->bsgp', uc, b_bar)
        a = jnp.broadcast_to(lam_bar, bu.shape)
        _, h = lax.associative_scan(_linear_recurrence, (a, bu), axis=1, reverse=(direction == 1))
        cmat = lax.complex(c_re[direction].astype(jnp.float32), c_im[direction].astype(jnp.float32))
        y = y + jnp.real(jnp.einsum('bsgp,ghp->bsgh', h, cmat))
    return y.reshape(bsz, s, D_SSM)


def expert_choice_ffn(x, w_router, w_gate, w_up, w_down):
    n_tok = x.shape[0] * x.shape[1]
    xt = x.reshape(n_tok, D_MODEL)
    aff = jax.nn.softmax(xt.astype(jnp.float32) @ w_router.astype(jnp.float32), axis=-1)
    cap = max(1, min(n_tok, EC_CAPACITY_FACTOR * n_tok // N_EXPERTS))
    gate, idx = lax.top_k(aff.T, cap)
    xe = xt[idx]
    h = jax.nn.silu(jnp.einsum('ecd,edf->ecf', xe, w_gate)) * jnp.einsum('ecd,edf->ecf', xe, w_up)
    ye = jnp.einsum('ecf,efd->ecd', h, w_down) * gate[..., None].astype(x.dtype)
    out = jnp.zeros_like(xt).at[idx.reshape(-1)].add(ye.reshape(-1, D_MODEL))
    return out.reshape(x.shape)


def encoder_layer(x, g_mix, w_in, g_q, g_k, attn_sink, lam_re, lam_im, log_step,
                  b_re, b_im, c_re, c_im, d_skip, w_glu, w_br_attn, w_br_ssm, w_out,
                  g_ffn, w_router, w_gate, w_up, w_down):
    b, s = x.shape[0], x.shape[1]
    xn = rms_norm(x, g_mix)
    proj = xn @ w_in
    q, k, v, u, gate_a, gate_s = jnp.split(proj, SPLITS, axis=-1)
    q = q.reshape(b, s, N_Q_HEADS, HEAD_DIM)
    k = k.reshape(b, s, N_KV_HEADS, HEAD_DIM)
    v = v.reshape(b, s, N_KV_HEADS, HEAD_DIM)
    attn = windowed_gqa(q, k, v, attn_sink, g_q, g_k)
    ssm = jax.nn.gelu(bidir_s5(u, lam_re, lam_im, log_step, b_re, b_im, c_re, c_im, d_skip)).astype(x.dtype)
    glu_val, glu_gate = jnp.split(ssm @ w_glu, 2, axis=-1)
    ssm_out = glu_val * jax.nn.sigmoid(glu_gate)
    merged = (jax.nn.sigmoid(gate_a) * (attn @ w_br_attn)
              + jax.nn.sigmoid(gate_s) * (ssm_out @ w_br_ssm))
    x = x + merged @ w_out
    x = x + expert_choice_ffn(rms_norm(x, g_ffn), w_router, w_gate, w_up, w_down)
    return x


def setup_inputs(seed: int = 0) -> dict:
    key = jax.random.key(seed)
    ks = jax.random.split(key, 24)
    f32 = jnp.float32
    L, G, P, H = DEPTH, N_SSM_GROUPS, SSM_STATE, SSM_GROUP
    nrm = lambda k, shape, scale: jax.random.normal(k, shape, f32) * scale
    lam_im_base = math.pi * jnp.arange(P, dtype=f32)
    return {
        "x_prompt": nrm(ks[0], (BATCH, SEQ, D_MODEL), 1.0),
        "x_sample": nrm(ks[1], (DEC_BATCH, DEC_SEQ, D_MODEL), 1.0),
        "g_mix": 1.0 + nrm(ks[2], (L, D_MODEL), 0.02),
        "w_in": nrm(ks[3], (L, D_MODEL, D_IN), D_MODEL ** -0.5),
        "g_q": 1.0 + nrm(ks[4], (L, HEAD_DIM), 0.02),
        "g_k": 1.0 + nrm(ks[5], (L, HEAD_DIM), 0.02),
        "attn_sink": nrm(ks[6], (L, N_Q_HEADS), 0.5),
        "lam_re": -0.5 + nrm(ks[7], (L, N_DIR, G, P), 0.01),
        "lam_im": lam_im_base + nrm(ks[8], (L, N_DIR, G, P), 0.01),
        "log_step": jax.random.uniform(ks[9], (L, N_DIR, G), f32, math.log(1e-3), math.log(1e-1)),
        "b_re": nrm(ks[10], (L, N_DIR, G, P, H), (2 * H) ** -0.5),
        "b_im": nrm(ks[11], (L, N_DIR, G, P, H), (2 * H) ** -0.5),
        "c_re": nrm(ks[12], (L, N_DIR, G, H, P), (2 * P) ** -0.5),
        "c_im": nrm(ks[13], (L, N_DIR, G, H, P), (2 * P) ** -0.5),
        "d_skip": 1.0 + nrm(ks[14], (L, D_SSM), 0.1),
        "w_glu": nrm(ks[15], (L, D_SSM, 2 * D_SSM), D_SSM ** -0.5),
        "w_br_attn": nrm(ks[16], (L, D_ATTN, D_MODEL), D_ATTN ** -0.5),
        "w_br_ssm": nrm(ks[17], (L, D_SSM, D_MODEL), D_SSM ** -0.5),
        "w_out": nrm(ks[18], (L, D_MODEL, D_MODEL), D_MODEL ** -0.5),
        "g_ffn": 1.0 + nrm(ks[19], (L, D_MODEL), 0.02),
        "w_router": nrm(ks[20], (L, D_MODEL, N_EXPERTS), D_MODEL ** -0.5),
        "w_gate": nrm(ks[21], (L, N_EXPERTS, D_MODEL, D_EXPERT), D_MODEL ** -0.5),
        "w_up": nrm(ks[22], (L, N_EXPERTS, D_MODEL, D_EXPERT), D_MODEL ** -0.5),
        "w_down": nrm(ks[23], (L, N_EXPERTS, D_EXPERT, D_MODEL), D_EXPERT ** -0.5),
    }


def reference(x_prompt, x_sample, g_mix, w_in, g_q, g_k, attn_sink, lam_re, lam_im, log_step,
              b_re, b_im, c_re, c_im, d_skip, w_glu, w_br_attn, w_br_ssm, w_out,
              g_ffn, w_router, w_gate, w_up, w_down):
    y_prompt = x_prompt
    y_sample = x_sample
    for l in range(DEPTH):
        layer_params = (g_mix[l], w_in[l], g_q[l], g_k[l], attn_sink[l], lam_re[l], lam_im[l],
                        log_step[l], b_re[l], b_im[l], c_re[l], c_im[l], d_skip[l], w_glu[l],
                        w_br_attn[l], w_br_ssm[l], w_out[l], g_ffn[l], w_router[l],
                        w_gate[l], w_up[l], w_down[l])
        y_prompt = encoder_layer(y_prompt, *layer_params)
        y_sample = encoder_layer(y_sample, *layer_params)
    return (y_prompt, y_sample)
```

```python
import functools
import math

import jax
import jax.numpy as jnp
from jax import lax
from jax.experimental import pallas as pl
from jax.experimental.pallas import tpu as pltpu

F32 = jnp.float32
BF16 = jnp.bfloat16

D_MODEL = 2048
N_Q_HEADS = 16
N_KV_HEADS = 4
HEAD_DIM = 128
Q_GROUP = N_Q_HEADS // N_KV_HEADS
D_ATTN = N_Q_HEADS * HEAD_DIM
D_KV = N_KV_HEADS * HEAD_DIM
WINDOW = 128
BLOCK = 128
D_SSM = 1024
SSM_GROUP = 16
N_SSM_GROUPS = 64
SSM_STATE = 64
N_EXPERTS = 16
EC_CAPACITY_FACTOR = 2
EPS = 1e-6
NEG_INF = -1e30

CHUNK = 128
LANES = 128
D_PROJ = D_ATTN + 2 * D_KV + 2 * D_MODEL
GATE_A_COL = D_ATTN // D_MODEL
K_COL = D_ATTN + 2 * D_MODEL
VMEM_LIMIT_BYTES = 56 * 1024 * 1024


def _cparams(sem):
    return pltpu.CompilerParams(dimension_semantics=sem, vmem_limit_bytes=VMEM_LIMIT_BYTES)


IN_TB = 512
IN_TN = 1024
IN_NJ = D_PROJ // IN_TN


def _inproj_kernel(x_ref, g_ref, w_ref, wut_ref, proj_ref, ut_ref, xn_scr):
    j = pl.program_id(1)

    @pl.when(j == 0)
    def _():
        x = x_ref[...]
        ms = jnp.mean(x * x, axis=-1, keepdims=True)
        xn_scr[...] = (x * lax.rsqrt(ms + EPS) * g_ref[...]).astype(BF16)

    @pl.when(j < IN_NJ)
    def _():
        proj_ref[...] = jnp.dot(xn_scr[...], w_ref[...], preferred_element_type=F32).astype(BF16)

    @pl.when(j == IN_NJ)
    def _():
        ut_ref[...] = lax.dot_general(wut_ref[...], xn_scr[...], (((1,), (1,)), ((), ())),
                                      preferred_element_type=F32).astype(BF16)


def _in_proj(x2d, g_mix, w_main, w_ut):
    n = x2d.shape[0]
    last = IN_NJ - 1
    return pl.pallas_call(
        _inproj_kernel,
        grid=(n // IN_TB, IN_NJ + 1),
        in_specs=[
            pl.BlockSpec((IN_TB, D_MODEL), lambda i, j: (i, 0)),
            pl.BlockSpec((1, D_MODEL), lambda i, j: (0, 0)),
            pl.BlockSpec((D_MODEL, IN_TN), lambda i, j: (0, jnp.minimum(j, last))),
            pl.BlockSpec((D_SSM, D_MODEL), lambda i, j: (0, 0)),
        ],
        out_specs=[
            pl.BlockSpec((IN_TB, IN_TN), lambda i, j: (i, jnp.minimum(j, last))),
            pl.BlockSpec((D_SSM, IN_TB), lambda i, j: (0, i)),
        ],
        out_shape=[jax.ShapeDtypeStruct((n, D_PROJ), BF16),
                   jax.ShapeDtypeStruct((D_SSM, n), BF16)],
        scratch_shapes=[pltpu.VMEM((IN_TB, D_MODEL), BF16)],
        compiler_params=_cparams(("arbitrary", "arbitrary")),
        name="in_proj",
    )(x2d, g_mix, w_main, w_ut)


KEYS = 3 * BLOCK
ALIBI_SLOPES = tuple(2.0 ** (-8.0 * (h + 1) / N_Q_HEADS) for h in range(N_Q_HEADS))


def _attn_kernel(sink_ref, q_ref, kp_ref, kc_ref, kn_ref, vp_ref, vc_ref, vn_ref, gq_ref, gk_ref,
                 o_ref, *, nb_seq):
    pos = pl.program_id(0) % nb_seq
    k_lo = jnp.where(pos == 0, BLOCK, 0)
    k_hi = jnp.where(pos == nb_seq - 1, 2 * BLOCK, KEYS)
    rows = Q_GROUP * BLOCK
    qi = lax.broadcasted_iota(jnp.int32, (rows, KEYS), 0) % BLOCK
    kj = lax.broadcasted_iota(jnp.int32, (rows, KEYS), 1)
    dist = jnp.abs(qi - kj + BLOCK)
    valid = (dist <= WINDOW) & (kj >= k_lo) & (kj < k_hi)
    distf = dist.astype(F32)
    head_of_row = lax.broadcasted_iota(jnp.int32, (rows, 1), 0) // BLOCK
    gq = gq_ref[...]
    gk = gk_ref[...]

    for kh in range(N_KV_HEADS):
        cs = slice(kh * HEAD_DIM, (kh + 1) * HEAD_DIM)
        k = jnp.concatenate([kp_ref[:, cs], kc_ref[:, cs], kn_ref[:, cs]], axis=0).astype(F32)
        kn = (k * lax.rsqrt(jnp.mean(k * k, axis=-1, keepdims=True) + EPS) * gk).astype(BF16)
        v = jnp.concatenate([vp_ref[:, cs], vc_ref[:, cs], vn_ref[:, cs]], axis=0)
        qs = []
        slope = jnp.zeros((rows, 1), F32)
        sink = jnp.zeros((rows, 1), F32)
        for j in range(Q_GROUP):
            h = kh * Q_GROUP + j
            q = q_ref[:, h * HEAD_DIM:(h + 1) * HEAD_DIM].astype(F32)
            qn = q * lax.rsqrt(jnp.mean(q * q, axis=-1, keepdims=True) + EPS) * gq * (HEAD_DIM ** -0.5)
            qs.append(qn.astype(BF16))
            slope = jnp.where(head_of_row == j, ALIBI_SLOPES[h], slope)
            sink = jnp.where(head_of_row == j, sink_ref[h], sink)
        qg = jnp.concatenate(qs, axis=0)
        s = lax.dot_general(qg, kn, (((1,), (1,)), ((), ())), preferred_element_type=F32)
        s = jnp.where(valid, s - slope * distf, NEG_INF)
        m = jnp.maximum(jnp.max(s, axis=-1, keepdims=True), sink)
        p = jnp.exp(s - m)
        den = jnp.sum(p, axis=-1, keepdims=True) + jnp.exp(sink - m)
        o = jnp.dot(p.astype(BF16), v, preferred_element_type=F32) / den
        for j in range(Q_GROUP):
            h = kh * Q_GROUP + j
            o_ref[:, h * HEAD_DIM:(h + 1) * HEAD_DIM] = o[j * BLOCK:(j + 1) * BLOCK].astype(BF16)


def _attention(proj, g_q, g_k, sink, seq):
    n = proj.shape[0]
    nb_seq = seq // BLOCK
    kcol = K_COL // D_KV
    vcol = kcol + 1

    def prev(i):
        return jnp.where(i % nb_seq == 0, i, i - 1)

    def nxt(i):
        return jnp.where(i % nb_seq == nb_seq - 1, i, i + 1)

    kv = lambda col, f: pl.BlockSpec((BLOCK, D_KV), lambda i: (f(i), col))
    same = lambda i: i
    return pl.pallas_call(
        functools.partial(_attn_kernel, nb_seq=nb_seq),
        grid=(n // BLOCK,),
        in_specs=[
            pl.BlockSpec(memory_space=pltpu.SMEM),
            pl.BlockSpec((BLOCK, D_ATTN), lambda i: (i, 0)),
            kv(kcol, prev), kv(kcol, same), kv(kcol, nxt),
            kv(vcol, prev), kv(vcol, same), kv(vcol, nxt),
            pl.BlockSpec((1, HEAD_DIM), lambda i: (0, 0)),
            pl.BlockSpec((1, HEAD_DIM), lambda i: (0, 0)),
        ],
        out_specs=pl.BlockSpec((BLOCK, D_ATTN), lambda i: (i, 0)),
        out_shape=jax.ShapeDtypeStruct((n, D_ATTN), BF16),
        compiler_params=_cparams(("arbitrary",)),
        name="attention",
    )(sink, proj, proj, proj, proj, proj, proj, proj, g_q, g_k)


H = SSM_GROUP
P = SSM_STATE
D_FLAT = H * CHUNK


def _cexp(tr, ti, k):
    mag = jnp.exp(tr * k)
    return mag * jnp.cos(ti * k), mag * jnp.sin(ti * k)


def _ssm_kernel(x_ref, prow_ref, pcol_ref, bt_ref, c_ref, ct_ref, dvec_ref, o_ref,
                kk_scr, m_scr, f_scr, e_scr, *, nc, lseq):
    prow = prow_ref[0]
    pcol = pcol_ref[0]
    bt = bt_ref[0]
    c = c_ref[0]
    ct = ct_ref[0]
    lag = lax.broadcasted_iota(jnp.int32, (1, CHUNK), 1).astype(F32)
    tpos = lax.broadcasted_iota(jnp.int32, (CHUNK, 1), 0).astype(F32)

    theta_rows = []
    for d in range(2):
        lr, li = prow[3 * d:3 * d + 1], prow[3 * d + 1:3 * d + 2]
        st = jnp.exp(prow[3 * d + 2:3 * d + 3])
        tr, ti = lr * st, li * st
        theta_rows.append((tr, ti))
        ar, ai = _cexp(tr, ti, 1.0)
        nr, ni = ar - 1.0, ai
        den = lr * lr + li * li
        cr, ci = (nr * lr + ni * li) / den, (ni * lr - nr * li) / den
        b_re, b_im = bt[2 * H * d:2 * H * d + H], bt[2 * H * d + H:2 * H * (d + 1)]
        bb_re, bb_im = cr * b_re - ci * b_im, cr * b_im + ci * b_re

        lrc, lic = pcol[:, 3 * d:3 * d + 1], pcol[:, 3 * d + 1:3 * d + 2]
        stc = jnp.exp(pcol[:, 3 * d + 2:3 * d + 3])
        trc, tic = lrc * stc, lic * stc

        cb_re, cb_im = [], []
        for ho in range(H):
            c_re, c_im = c[2 * H * d + ho:2 * H * d + ho + 1], c[2 * H * d + H + ho:2 * H * d + H + ho + 1]
            cb_re.append(c_re * bb_re - c_im * bb_im)
            cb_im.append(c_re * bb_im + c_im * bb_re)
        cb_re, cb_im = jnp.concatenate(cb_re, axis=0), jnp.concatenate(cb_im, axis=0)

        expo = lag if d == 0 else (CHUNK - lag)
        vr, vi = _cexp(trc, tic, expo)
        kt = jnp.dot(jnp.concatenate([cb_re, cb_im], axis=1), jnp.concatenate([vr, -vi], axis=0),
                     preferred_element_type=F32, precision=lax.Precision.HIGHEST)
        if d == 0:
            kk_scr[:, CHUNK:] = kt
        else:
            kk_scr[:, :CHUNK] = kt
            k0 = jnp.sum(cb_re, axis=1, keepdims=True)
            kk_scr[:, CHUNK:] = kk_scr[:, CHUNK:] + jnp.where(lag == 0.0, k0, 0.0)

        vsr, vsi = _cexp(tr, ti, (CHUNK - 1.0 - tpos) if d == 0 else tpos)
        for hi in range(H):
            fr = vsr * bb_re[hi:hi + 1] - vsi * bb_im[hi:hi + 1]
            fi = vsr * bb_im[hi:hi + 1] + vsi * bb_re[hi:hi + 1]
            f_scr[hi * CHUNK:(hi + 1) * CHUNK, 2 * P * d:2 * P * (d + 1)] = (
                jnp.concatenate([fr, fi], axis=1).astype(BF16))

        er, ei = _cexp(trc, tic, (lag + 1.0) if d == 0 else (CHUNK - lag))
        for ho in range(H):
            cc_re = ct[:, 2 * H * d + ho:2 * H * d + ho + 1]
            cc_im = ct[:, 2 * H * d + H + ho:2 * H * d + H + ho + 1]
            wr, wi = cc_re * er - cc_im * ei, cc_re * ei + cc_im * er
            e_scr[2 * P * d:2 * P * (d + 1), ho * CHUNK:(ho + 1) * CHUNK] = (
                jnp.concatenate([wr, -wi], axis=0).astype(BF16))

    def build(hi, carry):
        for ho in range(H):
            row = kk_scr[pl.ds(ho * H + hi, 1), :]
            rolled = pltpu.roll(jnp.broadcast_to(row, (CHUNK, 2 * CHUNK)), 0, 1, stride=1, stride_axis=0)
            m_scr[pl.ds(pl.multiple_of(hi * CHUNK, CHUNK), CHUNK), ho * CHUNK:(ho + 1) * CHUNK] = (
                rolled[:, CHUNK:].astype(BF16))
        return carry

    lax.fori_loop(0, H, build, 0)

    x = jnp.concatenate([x_ref[0, hi] for hi in range(H)], axis=1)
    summ = jnp.dot(x, f_scr[...], preferred_element_type=F32)
    pos = lax.broadcasted_iota(jnp.int32, (nc, 1), 0) % lseq

    def cmul_rows(tr, ti, k, s):
        ar, ai = _cexp(tr, ti, k)
        a_dup = jnp.concatenate([ar, ar], axis=1)
        a_sgn = jnp.concatenate([-ai, ai], axis=1)
        return a_dup * s + a_sgn * pltpu.roll(s, P, 1)

    carries = []
    for d in range(2):
        tr, ti = theta_rows[d]
        hs = summ[:, 2 * P * d:2 * P * (d + 1)]
        span = 1
        while span < lseq:
            if d == 0:
                sh = jnp.where(pos >= span, pltpu.roll(hs, span, 0), 0.0)
            else:
                sh = jnp.where(pos + span < lseq, pltpu.roll(hs, nc - span, 0), 0.0)
            hs = hs + cmul_rows(tr, ti, float(CHUNK * span), sh)
            span *= 2
        if d == 0:
            carries.append(jnp.where(pos >= 1, pltpu.roll(hs, 1, 0), 0.0))
        else:
            carries.append(jnp.where(pos + 1 < lseq, pltpu.roll(hs, nc - 1, 0), 0.0))

    y = jnp.dot(x, m_scr[...], preferred_element_type=F32)
    y = y + jnp.dot(jnp.concatenate(carries, axis=1).astype(BF16), e_scr[...], preferred_element_type=F32)
    y = y + dvec_ref[0] * x.astype(F32)
    y = jax.nn.gelu(y)
    for ho in range(H):
        o_ref[0, ho] = y[:, ho * CHUNK:(ho + 1) * CHUNK].astype(BF16)


def _ssm(u4, prow, pcol, bt, c, ct, dvec, seq):
    g, _, nc, _ = u4.shape
    lseq = seq // CHUNK
    per_g = lambda *blk: pl.BlockSpec((1,) + blk, lambda i: (i,) + (0,) * len(blk))
    return pl.pallas_call(
        functools.partial(_ssm_kernel, nc=nc, lseq=lseq),
        grid=(g,),
        in_specs=[per_g(H, nc, CHUNK), per_g(8, P), per_g(P, 8), per_g(4 * H, P), per_g(4 * H, P),
                  per_g(P, 4 * H), per_g(1, D_FLAT)],
        out_specs=per_g(H, nc, CHUNK),
        out_shape=jax.ShapeDtypeStruct(u4.shape, BF16),
        scratch_shapes=[pltpu.VMEM((H * H, 2 * CHUNK), F32),
                        pltpu.VMEM((D_FLAT, D_FLAT), BF16),
                        pltpu.VMEM((D_FLAT, 4 * P), BF16),
                        pltpu.VMEM((4 * P, D_FLAT), BF16)],
        compiler_params=_cparams(("arbitrary",)),
        name="ssm",
    )(u4, prow, pcol, bt, c, ct, dvec)


MIX_TB = 256


def _mix_kernel(attn_ref, g_ref, ga_ref, gs_ref, wglu_ref, wa_ref, ws_ref, o_ref):
    glu = jnp.dot(g_ref[...], wglu_ref[...], preferred_element_type=F32)
    ssm_out = (glu[:, :D_SSM] * jax.nn.sigmoid(glu[:, D_SSM:])).astype(BF16)
    a = jnp.dot(attn_ref[...], wa_ref[...], preferred_element_type=F32)
    s = jnp.dot(ssm_out, ws_ref[...], preferred_element_type=F32)
    merged = jax.nn.sigmoid(ga_ref[...].astype(F32)) * a + jax.nn.sigmoid(gs_ref[...].astype(F32)) * s
    o_ref[...] = merged.astype(BF16)


def _mix(attn, g, proj, w_glu, w_a, w_s):
    n = attn.shape[0]
    full = lambda r, c_: pl.BlockSpec((r, c_), lambda i: (0, 0))
    return pl.pallas_call(
        _mix_kernel,
        grid=(n // MIX_TB,),
        in_specs=[
            pl.BlockSpec((MIX_TB, D_ATTN), lambda i: (i, 0)),
            pl.BlockSpec((MIX_TB, D_SSM), lambda i: (i, 0)),
            pl.BlockSpec((MIX_TB, D_MODEL), lambda i: (i, GATE_A_COL)),
            pl.BlockSpec((MIX_TB, D_MODEL), lambda i: (i, GATE_A_COL + 1)),
            full(D_SSM, 2 * D_SSM), full(D_ATTN, D_MODEL), full(D_SSM, D_MODEL),
        ],
        out_specs=pl.BlockSpec((MIX_TB, D_MODEL), lambda i: (i, 0)),
        out_shape=jax.ShapeDtypeStruct((n, D_MODEL), BF16),
        compiler_params=_cparams(("arbitrary",)),
        name="mix",
    )(attn, g, proj, proj, w_glu, w_a, w_s)


OUT_TB = 256


def _out_kernel(m_ref, x_ref, wout_ref, gffn_ref, wrt_ref, x2_ref, xn_ref, afft_ref):
    x2 = x_ref[...] + jnp.dot(m_ref[...], wout_ref[...], preferred_element_type=F32)
    x2_ref[...] = x2
    xn = x2 * lax.rsqrt(jnp.mean(x2 * x2, axis=-1, keepdims=True) + EPS) * gffn_ref[...]
    xn_ref[...] = xn
    logits = lax.dot_general(wrt_ref[...], xn.astype(BF16), (((1,), (1,)), ((), ())),
                             preferred_element_type=F32)
    ex = jnp.exp(logits - jnp.max(logits, axis=0, keepdims=True))
    afft_ref[...] = ex / jnp.sum(ex, axis=0, keepdims=True)


def _out_proj(merged, x2d, w_out, g_ffn, w_rt):
    n = x2d.shape[0]
    return pl.pallas_call(
        _out_kernel,
        grid=(n // OUT_TB,),
        in_specs=[
            pl.BlockSpec((OUT_TB, D_MODEL), lambda i: (i, 0)),
            pl.BlockSpec((OUT_TB, D_MODEL), lambda i: (i, 0)),
            pl.BlockSpec((D_MODEL, D_MODEL), lambda i: (0, 0)),
            pl.BlockSpec((1, D_MODEL), lambda i: (0, 0)),
            pl.BlockSpec((N_EXPERTS, D_MODEL), lambda i: (0, 0)),
        ],
        out_specs=[
            pl.BlockSpec((OUT_TB, D_MODEL), lambda i: (i, 0)),
            pl.BlockSpec((OUT_TB, D_MODEL), lambda i: (i, 0)),
            pl.BlockSpec((N_EXPERTS, OUT_TB), lambda i: (0, i)),
        ],
        out_shape=[jax.ShapeDtypeStruct((n, D_MODEL), F32),
                   jax.ShapeDtypeStruct((n, D_MODEL), F32),
                   jax.ShapeDtypeStruct((N_EXPERTS, n), F32)],
        compiler_params=_cparams(("arbitrary",)),
        name="out_proj",
    )(merged, x2d, w_out, g_ffn, w_rt)


F32_INF_BITS = 0x7F800000


def _route_kernel(aff_ref, idx_ref, gate_ref, *, cap):
    aff = aff_ref[0]
    nb = aff.shape[0]
    bits = pltpu.bitcast(aff, jnp.int32)
    tok = (lax.broadcasted_iota(jnp.int32, (nb, LANES), 0) * LANES
           + lax.broadcasted_iota(jnp.int32, (nb, LANES), 1))

    def count(mask):
        return jnp.sum(jnp.where(mask, 1.0, 0.0), keepdims=True)

    def vstep(_, lohi):
        lo, hi = lohi
        mid = lo + ((hi - lo + 1) >> 1)
        ok = count(bits >= mid) >= cap
        return jnp.where(ok, mid, lo), jnp.where(ok, hi, mid - 1)

    thr, _ = lax.fori_loop(0, 32, vstep, (jnp.zeros((1, 1), jnp.int32),
                                          jnp.full((1, 1), F32_INF_BITS, jnp.int32)))
    above = bits > thr
    tied = bits == thr
    need = cap - count(above)

    def tstep(_, lohi):
        lo, hi = lohi
        mid = lo + ((hi - lo) >> 1)
        ok = count(tied & (tok <= mid)) >= need
        return jnp.where(ok, lo, mid + 1), jnp.where(ok, mid, hi)

    cut, _ = lax.fori_loop(0, 32, tstep, (jnp.zeros((1, 1), jnp.int32),
                                          jnp.full((1, 1), nb * LANES - 1, jnp.int32)))
    sel = jnp.where(above | (tied & (tok <= cut)), 1.0, 0.0).astype(BF16)

    r_i = lax.broadcasted_iota(jnp.int32, (LANES, LANES), 0)
    c_i = lax.broadcasted_iota(jnp.int32, (LANES, LANES), 1)
    upper = jnp.where(r_i <= c_i, 1.0, 0.0).astype(BF16)
    cw = jnp.dot(sel, upper, preferred_element_type=F32)
    cnt_row = lax.dot_general(jnp.ones((8, LANES), BF16), sel, (((1,), (1,)), ((), ())),
                              preferred_element_type=F32)
    rb = lax.broadcasted_iota(jnp.int32, (nb, nb), 0)
    cb = lax.broadcasted_iota(jnp.int32, (nb, nb), 1)
    upper_nb = jnp.where(rb <= cb, 1.0, 0.0).astype(BF16)
    incl_row = jnp.dot(cnt_row.astype(BF16), upper_nb, preferred_element_type=F32)[0:1]
    excl_row = incl_row - cnt_row[0:1]

    slot = lax.broadcasted_iota(jnp.int32, (cap, 1), 0).astype(F32)
    blk = jnp.sum(jnp.where(incl_row <= slot, 1.0, 0.0), axis=1, keepdims=True)
    onehot_b = lax.broadcasted_iota(jnp.int32, (cap, nb), 1).astype(F32) == blk
    base = jnp.sum(jnp.where(onehot_b, excl_row, 0.0), axis=1, keepdims=True)
    oh = jnp.where(onehot_b, 1.0, 0.0).astype(BF16)
    cw_j = jnp.dot(oh, cw.astype(BF16), preferred_element_type=F32)
    loc = jnp.sum(jnp.where(cw_j <= slot - base, 1.0, 0.0), axis=1, keepdims=True)
    idx = (blk * LANES + loc).astype(jnp.int32)
    idx_ref[0] = jnp.broadcast_to(idx, (cap, LANES))

    a_hi = aff.astype(BF16)
    r1 = aff - a_hi.astype(F32)
    a_mid = r1.astype(BF16)
    a_lo = (r1 - a_mid.astype(F32)).astype(BF16)
    aff_j = (jnp.dot(oh, a_hi, preferred_element_type=F32) + jnp.dot(oh, a_mid, preferred_element_type=F32)
             + jnp.dot(oh, a_lo, preferred_element_type=F32))
    lane = lax.broadcasted_iota(jnp.int32, (cap, LANES), 1).astype(F32)
    gate = jnp.sum(jnp.where(lane == loc, aff_j, 0.0), axis=1, keepdims=True)
    gate_ref[0] = jnp.broadcast_to(gate, (cap, LANES))


def _route(aff3, cap):
    e, nb, _ = aff3.shape
    return pl.pallas_call(
        functools.partial(_route_kernel, cap=cap),
        grid=(e,),
        in_specs=[pl.BlockSpec((1, nb, LANES), lambda i: (i, 0, 0))],
        out_specs=[pl.BlockSpec((1, cap, LANES), lambda i: (i, 0, 0)),
                   pl.BlockSpec((1, cap, LANES), lambda i: (i, 0, 0))],
        out_shape=[jax.ShapeDtypeStruct((e, cap, LANES), jnp.int32),
                   jax.ShapeDtypeStruct((e, cap, LANES), F32)],
        compiler_params=_cparams(("arbitrary",)),
        name="route",
    )(aff3)


FFN_TM = 256


def _ffn_kernel(idx_ref, gate_ref, xn_hbm, acc_in_hbm, wg_hbm, wu_hbm, wd_hbm, acc_hbm,
                w_scr, xbuf, abuf, sems, *, tm):
    del acc_in_hbm
    e = pl.program_id(0)

    @pl.when(pl.program_id(1) == 0)
    def _():
        copies = [pltpu.make_async_copy(w.at[e], w_scr.at[k], sems.at[k])
                  for k, w in enumerate((wg_hbm, wu_hbm, wd_hbm))]
        for cp in copies:
            cp.start()
        for cp in copies:
            cp.wait()

    def row_in(j):
        t = idx_ref[0, 0, 0, j]
        return (pltpu.make_async_copy(xn_hbm.at[pl.ds(t, 1), :], xbuf.at[pl.ds(j, 1), :], sems.at[3]),
                pltpu.make_async_copy(acc_hbm.at[pl.ds(t, 1), :], abuf.at[pl.ds(j, 1), :], sems.at[4]))

    def row_out(j):
        t = idx_ref[0, 0, 0, j]
        return pltpu.make_async_copy(abuf.at[pl.ds(j, 1), :], acc_hbm.at[pl.ds(t, 1), :], sems.at[5])

    def start_in(j, c_):
        for cp in row_in(j):
            cp.start()
        return c_

    def wait_in(j, c_):
        for cp in row_in(j):
            cp.wait()
        return c_

    lax.fori_loop(0, tm, start_in, 0)
    lax.fori_loop(0, tm, wait_in, 0)

    x = xbuf[...].astype(BF16)
    hid = (jax.nn.silu(jnp.dot(x, w_scr[0], preferred_element_type=F32))
           * jnp.dot(x, w_scr[1], preferred_element_type=F32))
    y = jnp.dot(hid.astype(BF16), w_scr[2], preferred_element_type=F32)
    abuf[...] = abuf[...] + y * gate_ref[0][:, 0:1]

    def start_out(j, c_):
        row_out(j).start()
        return c_

    def wait_out(j, c_):
        row_out(j).wait()
        return c_

    lax.fori_loop(0, tm, start_out, 0)
    lax.fori_loop(0, tm, wait_out, 0)


def _ffn(idx3, gate, xn, acc, w_gate, w_up, w_down):
    e, nt, _, tm = idx3.shape
    anyspec = pl.BlockSpec(memory_space=pl.ANY)
    return pl.pallas_call(
        functools.partial(_ffn_kernel, tm=tm),
        grid=(e, nt),
        in_specs=[
            pl.BlockSpec((1, 1, 1, tm), lambda i, r: (i, r, 0, 0), memory_space=pltpu.SMEM),
            pl.BlockSpec((1, tm, LANES), lambda i, r: (i, r, 0)),
            anyspec, anyspec, anyspec, anyspec, anyspec,
        ],
        out_specs=anyspec,
        out_shape=jax.ShapeDtypeStruct(acc.shape, F32),
        scratch_shapes=[pltpu.VMEM((3, D_MODEL, D_MODEL), BF16),
                        pltpu.VMEM((tm, D_MODEL), F32),
                        pltpu.VMEM((tm, D_MODEL), F32),
                        pltpu.SemaphoreType.DMA((6,))],
        input_output_aliases={3: 0},
        compiler_params=_cparams(("arbitrary", "arbitrary")),
        name="ffn",
    )(idx3, gate, xn, acc, w_gate, w_up, w_down)


def _prep_weights(g_mix, w_in, g_q, g_k, attn_sink, lam_re, lam_im, log_step, b_re, b_im, c_re, c_im,
                  d_skip, w_glu, w_br_attn, w_br_ssm, w_out, g_ffn, w_router, w_gate, w_up, w_down):
    u0 = D_ATTN + 2 * D_KV
    w_main = jnp.concatenate([w_in[:, :D_ATTN], w_in[:, u0 + D_SSM:], w_in[:, D_ATTN:u0]], axis=1).astype(BF16)
    w_ut = w_in[:, u0:u0 + D_SSM].T.astype(BF16)
    g_ = N_SSM_GROUPS
    ls = jnp.broadcast_to(log_step[:, :, None], (2, g_, P))
    zeros = jnp.zeros((g_, P), F32)
    prow = jnp.stack([lam_re[0], lam_im[0], ls[0], lam_re[1], lam_im[1], ls[1], zeros, zeros], axis=1)
    pcol = jnp.swapaxes(prow, 1, 2)
    tr = lambda a: jnp.swapaxes(a, 1, 2)
    bt = jnp.concatenate([tr(b_re[0]), tr(b_im[0]), tr(b_re[1]), tr(b_im[1])], axis=1)
    c = jnp.concatenate([c_re[0], c_im[0], c_re[1], c_im[1]], axis=1)
    ct = jnp.swapaxes(c, 1, 2)
    dvec = jnp.repeat(d_skip.reshape(g_, H), CHUNK, axis=1).reshape(g_, 1, D_FLAT)
    return dict(
        g_mix=g_mix.reshape(1, D_MODEL), w_main=w_main, w_ut=w_ut,
        g_q=g_q.reshape(1, HEAD_DIM), g_k=g_k.reshape(1, HEAD_DIM), sink=attn_sink,
        prow=prow, pcol=pcol, bt=bt, c=c, ct=ct, dvec=dvec,
        w_glu=w_glu.astype(BF16), w_a=w_br_attn.astype(BF16), w_s=w_br_ssm.astype(BF16),
        w_out=w_out.astype(BF16), g_ffn=g_ffn.reshape(1, D_MODEL), w_rt=w_router.T.astype(BF16),
        w_gate=w_gate.astype(BF16), w_up=w_up.astype(BF16), w_down=w_down.astype(BF16))


def _layer(x, w):
    b, s, _ = x.shape
    n = b * s
    x2d = x.reshape(n, D_MODEL)
    proj, ut = _in_proj(x2d, w["g_mix"], w["w_main"], w["w_ut"])
    attn = _attention(proj, w["g_q"], w["g_k"], w["sink"], s)
    u4 = ut.reshape(N_SSM_GROUPS, H, n // CHUNK, CHUNK)
    g4 = _ssm(u4, w["prow"], w["pcol"], w["bt"], w["c"], w["ct"], w["dvec"], s)
    g = g4.reshape(D_SSM, n).T
    merged = _mix(attn, g, proj, w["w_glu"], w["w_a"], w["w_s"])
    x2, xn, afft = _out_proj(merged, x2d, w["w_out"], w["g_ffn"], w["w_rt"])
    cap = max(1, min(n, EC_CAPACITY_FACTOR * n // N_EXPERTS))
    idx, gate = _route(afft.reshape(N_EXPERTS, n // LANES, LANES), cap)
    tm = min(FFN_TM, cap)
    idx3 = idx[:, :, 0].reshape(N_EXPERTS, cap // tm, 1, tm)
    y = _ffn(idx3, gate, xn, x2, w["w_gate"], w["w_up"], w["w_down"])
    return y.reshape(b, s, D_MODEL)


def kernel(x_prompt, x_sample, g_mix, w_in, g_q, g_k, attn_sink, lam_re, lam_im, log_step, b_re, b_im,
           c_re, c_im, d_skip, w_glu, w_br_attn, w_br_ssm, w_out, g_ffn, w_router, w_gate, w_up, w_down):
    depth = g_mix.shape[0]
    y_prompt, y_sample = x_prompt, x_sample
    for l in range(depth):
        w = _prep_weights(g_mix[l], w_in[l], g_q[l], g_k[l], attn_sink[l], lam_re[l], lam_im[l],
                          log_step[l], b_re[l], b_im[l], c_re[l], c_im[l], d_skip[l], w_glu[l],
                          w_br_attn[l], w_br_ssm[l], w_out[l], g_ffn[l], w_router[l], w_gate[l],
                          w_up[l], w_down[l])
        y_prompt = _layer(y_prompt, w)
        y_sample = _layer(y_sample, w)
    return (y_prompt, y_sample)
```

```python
import functools
import math

import jax
import jax.numpy as jnp
from jax import lax
from jax.experimental import pallas as pl
from jax.experimental.pallas import tpu as pltpu

F32 = jnp.float32
BF16 = jnp.bfloat16

D_MODEL = 2048
N_Q_HEADS = 16
N_KV_HEADS = 4
HEAD_DIM = 128
Q_GROUP = N_Q_HEADS // N_KV_HEADS
D_ATTN = N_Q_HEADS * HEAD_DIM
D_KV = N_KV_HEADS * HEAD_DIM
WINDOW = 128
BLOCK = 128
D_SSM = 1024
SSM_GROUP = 16
N_SSM_GROUPS = 64
SSM_STATE = 64
N_EXPERTS = 16
EC_CAPACITY_FACTOR = 2
EPS = 1e-6
NEG_INF = -1e30

CHUNK = 128
LANES = 128
D_PROJ = D_ATTN + 2 * D_KV + 2 * D_MODEL
GATE_A_COL = D_ATTN // D_MODEL
K_COL = D_ATTN + 2 * D_MODEL
VMEM_LIMIT_BYTES = 56 * 1024 * 1024


def _cparams(sem):
    return pltpu.CompilerParams(dimension_semantics=sem, vmem_limit_bytes=VMEM_LIMIT_BYTES)


IN_TB = 512
IN_TN = 1024
IN_NJ = D_PROJ // IN_TN


def _inproj_kernel(x_ref, g_ref, w_ref, wut_ref, proj_ref, ut_ref, xn_scr):
    j = pl.program_id(1)

    @pl.when(j == 0)
    def _():
        x = x_ref[...]
        ms = jnp.mean(x * x, axis=-1, keepdims=True)
        xn_scr[...] = (x * lax.rsqrt(ms + EPS) * g_ref[...]).astype(BF16)

    @pl.when(j < IN_NJ)
    def _():
        proj_ref[...] = jnp.dot(xn_scr[...], w_ref[...], preferred_element_type=F32).astype(BF16)

    @pl.when(j == IN_NJ)
    def _():
        ut_ref[...] = lax.dot_general(wut_ref[...], xn_scr[...], (((1,), (1,)), ((), ())),
                                      preferred_element_type=F32).astype(BF16)


def _in_proj(x2d, g_mix, w_main, w_ut):
    n = x2d.shape[0]
    last = IN_NJ - 1
    return pl.pallas_call(
        _inproj_kernel,
        grid=(n // IN_TB, IN_NJ + 1),
        in_specs=[
            pl.BlockSpec((IN_TB, D_MODEL), lambda i, j: (i, 0)),
            pl.BlockSpec((1, D_MODEL), lambda i, j: (0, 0)),
            pl.BlockSpec((D_MODEL, IN_TN), lambda i, j: (0, jnp.minimum(j, last))),
            pl.BlockSpec((D_SSM, D_MODEL), lambda i, j: (0, 0)),
        ],
        out_specs=[
            pl.BlockSpec((IN_TB, IN_TN), lambda i, j: (i, jnp.minimum(j, last))),
            pl.BlockSpec((D_SSM, IN_TB), lambda i, j: (0, i)),
        ],
        out_shape=[jax.ShapeDtypeStruct((n, D_PROJ), BF16),
                   jax.ShapeDtypeStruct((D_SSM, n), BF16)],
        scratch_shapes=[pltpu.VMEM((IN_TB, D_MODEL), BF16)],
        compiler_params=_cparams(("arbitrary", "arbitrary")),
        name="in_proj",
    )(x2d, g_mix, w_main, w_ut)


KEYS = 3 * BLOCK
ALIBI_SLOPES = tuple(2.0 ** (-8.0 * (h + 1) / N_Q_HEADS) for h in range(N_Q_HEADS))


def _attn_kernel(sink_ref, q_ref, kp_ref, kc_ref, kn_ref, vp_ref, vc_ref, vn_ref, gq_ref, gk_ref,
                 o_ref, *, nb_seq):
    pos = pl.program_id(0) % nb_seq
    k_lo = jnp.where(pos == 0, BLOCK, 0)
    k_hi = jnp.where(pos == nb_seq - 1, 2 * BLOCK, KEYS)
    rows = Q_GROUP * BLOCK
    qi = lax.broadcasted_iota(jnp.int32, (rows, KEYS), 0) % BLOCK
    kj = lax.broadcasted_iota(jnp.int32, (rows, KEYS), 1)
    dist = jnp.abs(qi - kj + BLOCK)
    valid = (dist <= WINDOW) & (kj >= k_lo) & (kj < k_hi)
    distf = dist.astype(F32)
    head_of_row = lax.broadcasted_iota(jnp.int32, (rows, 1), 0) // BLOCK
    gq = gq_ref[...]
    gk = gk_ref[...]

    for kh in range(N_KV_HEADS):
        cs = slice(kh * HEAD_DIM, (kh + 1) * HEAD_DIM)
        k = jnp.concatenate([kp_ref[:, cs], kc_ref[:, cs], kn_ref[:, cs]], axis=0).astype(F32)
        kn = (k * lax.rsqrt(jnp.mean(k * k, axis=-1, keepdims=True) + EPS) * gk).astype(BF16)
        v = jnp.concatenate([vp_ref[:, cs], vc_ref[:, cs], vn_ref[:, cs]], axis=0)
        qs = []
        slope = jnp.zeros((rows, 1), F32)
        sink = jnp.zeros((rows, 1), F32)
        for j in range(Q_GROUP):
            h = kh * Q_GROUP + j
            q = q_ref[:, h * HEAD_DIM:(h + 1) * HEAD_DIM].astype(F32)
            qn = q * lax.rsqrt(jnp.mean(q * q, axis=-1, keepdims=True) + EPS) * gq * (HEAD_DIM ** -0.5)
            qs.append(qn.astype(BF16))
            slope = jnp.where(head_of_row == j, ALIBI_SLOPES[h], slope)
            sink = jnp.where(head_of_row == j, sink_ref[h], sink)
        qg = jnp.concatenate(qs, axis=0)
        s = lax.dot_general(qg, kn, (((1,), (1,)), ((), ())), preferred_element_type=F32)
        s = jnp.where(valid, s - slope * distf, NEG_INF)
        m = jnp.maximum(jnp.max(s, axis=-1, keepdims=True), sink)
        p = jnp.exp(s - m)
        den = jnp.sum(p, axis=-1, keepdims=True) + jnp.exp(sink - m)
        o = jnp.dot(p.astype(BF16), v, preferred_element_type=F32) / den
        for j in range(Q_GROUP):
            h = kh * Q_GROUP + j
            o_ref[:, h * HEAD_DIM:(h + 1) * HEAD_DIM] = o[j * BLOCK:(j + 1) * BLOCK].astype(BF16)


def _attention(proj, g_q, g_k, sink, seq):
    n = proj.shape[0]
    nb_seq = seq // BLOCK
    kcol = K_COL // D_KV
    vcol = kcol + 1

    def prev(i):
        return jnp.where(i % nb_seq == 0, i, i - 1)

    def nxt(i):
        return jnp.where(i % nb_seq == nb_seq - 1, i, i + 1)

    kv = lambda col, f: pl.BlockSpec((BLOCK, D_KV), lambda i: (f(i), col))
    same = lambda i: i
    return pl.pallas_call(
        functools.partial(_attn_kernel, nb_seq=nb_seq),
        grid=(n // BLOCK,),
        in_specs=[
            pl.BlockSpec(memory_space=pltpu.SMEM),
            pl.BlockSpec((BLOCK, D_ATTN), lambda i: (i, 0)),
            kv(kcol, prev), kv(kcol, same), kv(kcol, nxt),
            kv(vcol, prev), kv(vcol, same), kv(vcol, nxt),
            pl.BlockSpec((1, HEAD_DIM), lambda i: (0, 0)),
            pl.BlockSpec((1, HEAD_DIM), lambda i: (0, 0)),
        ],
        out_specs=pl.BlockSpec((BLOCK, D_ATTN), lambda i: (i, 0)),
        out_shape=jax.ShapeDtypeStruct((n, D_ATTN), BF16),
        compiler_params=_cparams(("arbitrary",)),
        name="attention",
    )(sink, proj, proj, proj, proj, proj, proj, proj, g_q, g_k)


H = SSM_GROUP
P = SSM_STATE
D_FLAT = H * CHUNK


def _cexp(tr, ti, k):
    mag = jnp.exp(tr * k)
    return mag * jnp.cos(ti * k), mag * jnp.sin(ti * k)


def _ssm_kernel(x_ref, prow_ref, pcol_ref, bt_ref, c_ref, ct_ref, dvec_ref, o_ref,
                kk_scr, m_scr, f_scr, e_scr, *, nc, lseq):
    prow = prow_ref[0]
    pcol = pcol_ref[0]
    bt = bt_ref[0]
    c = c_ref[0]
    ct = ct_ref[0]
    lag = lax.broadcasted_iota(jnp.int32, (1, CHUNK), 1).astype(F32)
    tpos = lax.broadcasted_iota(jnp.int32, (CHUNK, 1), 0).astype(F32)

    theta_rows = []
    for d in range(2):
        lr, li = prow[3 * d:3 * d + 1], prow[3 * d + 1:3 * d + 2]
        st = jnp.exp(prow[3 * d + 2:3 * d + 3])
        tr, ti = lr * st, li * st
        theta_rows.append((tr, ti))
        ar, ai = _cexp(tr, ti, 1.0)
        nr, ni = ar - 1.0, ai
        den = lr * lr + li * li
        cr, ci = (nr * lr + ni * li) / den, (ni * lr - nr * li) / den
        b_re, b_im = bt[2 * H * d:2 * H * d + H], bt[2 * H * d + H:2 * H * (d + 1)]
        bb_re, bb_im = cr * b_re - ci * b_im, cr * b_im + ci * b_re

        lrc, lic = pcol[:, 3 * d:3 * d + 1], pcol[:, 3 * d + 1:3 * d + 2]
        stc = jnp.exp(pcol[:, 3 * d + 2:3 * d + 3])
        trc, tic = lrc * stc, lic * stc

        cb_re, cb_im = [], []
        for ho in range(H):
            c_re, c_im = c[2 * H * d + ho:2 * H * d + ho + 1], c[2 * H * d + H + ho:2 * H * d + H + ho + 1]
            cb_re.append(c_re * bb_re - c_im * bb_im)
            cb_im.append(c_re * bb_im + c_im * bb_re)
        cb_re, cb_im = jnp.concatenate(cb_re, axis=0), jnp.concatenate(cb_im, axis=0)

        expo = lag if d == 0 else (CHUNK - lag)
        vr, vi = _cexp(trc, tic, expo)
        kt = jnp.dot(jnp.concatenate([cb_re, cb_im], axis=1), jnp.concatenate([vr, -vi], axis=0),
                     preferred_element_type=F32, precision=lax.Precision.HIGHEST)
        if d == 0:
            kk_scr[:, CHUNK:] = kt
        else:
            kk_scr[:, :CHUNK] = kt
            k0 = jnp.sum(cb_re, axis=1, keepdims=True)
            kk_scr[:, CHUNK:] = kk_scr[:, CHUNK:] + jnp.where(lag == 0.0, k0, 0.0)

        vsr, vsi = _cexp(tr, ti, (CHUNK - 1.0 - tpos) if d == 0 else tpos)
        for hi in range(H):
            fr = vsr * bb_re[hi:hi + 1] - vsi * bb_im[hi:hi + 1]
            fi = vsr * bb_im[hi:hi + 1] + vsi * bb_re[hi:hi + 1]
            f_scr[hi * CHUNK:(hi + 1) * CHUNK, 2 * P * d:2 * P * (d + 1)] = (
                jnp.concatenate([fr, fi], axis=1).astype(BF16))

        er, ei = _cexp(trc, tic, (lag + 1.0) if d == 0 else (CHUNK - lag))
        for ho in range(H):
            cc_re = ct[:, 2 * H * d + ho:2 * H * d + ho + 1]
            cc_im = ct[:, 2 * H * d + H + ho:2 * H * d + H + ho + 1]
            wr, wi = cc_re * er - cc_im * ei, cc_re * ei + cc_im * er
            e_scr[2 * P * d:2 * P * (d + 1), ho * CHUNK:(ho + 1) * CHUNK] = (
                jnp.concatenate([wr, -wi], axis=0).astype(BF16))

    def build(hi, carry):
        for ho in range(H):
            row = kk_scr[pl.ds(ho * H + hi, 1), :]
            rolled = pltpu.roll(jnp.broadcast_to(row, (CHUNK, 2 * CHUNK)), 0, 1, stride=1, stride_axis=0)
            m_scr[pl.ds(pl.multiple_of(hi * CHUNK, CHUNK), CHUNK), ho * CHUNK:(ho + 1) * CHUNK] = (
                rolled[:, CHUNK:].astype(BF16))
        return carry

    lax.fori_loop(0, H, build, 0)

    x = jnp.concatenate([x_ref[0, hi] for hi in range(H)], axis=1)
    summ = jnp.dot(x, f_scr[...], preferred_element_type=F32)
    pos = lax.broadcasted_iota(jnp.int32, (nc, 1), 0) % lseq

    def cmul_rows(tr, ti, k, s):
        ar, ai = _cexp(tr, ti, k)
        a_dup = jnp.concatenate([ar, ar], axis=1)
        a_sgn = jnp.concatenate([-ai, ai], axis=1)
        return a_dup * s + a_sgn * pltpu.roll(s, P, 1)

    carries = []
    for d in range(2):
        tr, ti = theta_rows[d]
        hs = summ[:, 2 * P * d:2 * P * (d + 1)]
        span = 1
        while span < lseq:
            if d == 0:
                sh = jnp.where(pos >= span, pltpu.roll(hs, span, 0), 0.0)
            else:
                sh = jnp.where(pos + span < lseq, pltpu.roll(hs, nc - span, 0), 0.0)
            hs = hs + cmul_rows(tr, ti, float(CHUNK * span), sh)
            span *= 2
        if d == 0:
            carries.append(jnp.where(pos >= 1, pltpu.roll(hs, 1, 0), 0.0))
        else:
            carries.append(jnp.where(pos + 1 < lseq, pltpu.roll(hs, nc - 1, 0), 0.0))

    y = jnp.dot(x, m_scr[...], preferred_element_type=F32)
    y = y + jnp.dot(jnp.concatenate(carries, axis=1).astype(BF16), e_scr[...], preferred_element_type=F32)
    y = y + dvec_ref[0] * x.astype(F32)
    y = jax.nn.gelu(y)
    for ho in range(H):
        o_ref[0, ho] = y[:, ho * CHUNK:(ho + 1) * CHUNK].astype(BF16)


def _ssm(u4, prow, pcol, bt, c, ct, dvec, seq):
    g, _, nc, _ = u4.shape
    lseq = seq // CHUNK
    per_g = lambda *blk: pl.BlockSpec((1,) + blk, lambda i: (i,) + (0,) * len(blk))
    return pl.pallas_call(
        functools.partial(_ssm_kernel, nc=nc, lseq=lseq),
        grid=(g,),
        in_specs=[per_g(H, nc, CHUNK), per_g(8, P), per_g(P, 8), per_g(4 * H, P), per_g(4 * H, P),
                  per_g(P, 4 * H), per_g(1, D_FLAT)],
        out_specs=per_g(H, nc, CHUNK),
        out_shape=jax.ShapeDtypeStruct(u4.shape, BF16),
        scratch_shapes=[pltpu.VMEM((H * H, 2 * CHUNK), F32),
                        pltpu.VMEM((D_FLAT, D_FLAT), BF16),
                        pltpu.VMEM((D_FLAT, 4 * P), BF16),
                        pltpu.VMEM((4 * P, D_FLAT), BF16)],
        compiler_params=_cparams(("arbitrary",)),
        name="ssm",
    )(u4, prow, pcol, bt, c, ct, dvec)


MIX_TB = 256


def _mix_kernel(attn_ref, g_ref, ga_ref, gs_ref, wglu_ref, wa_ref, ws_ref, o_ref):
    glu = jnp.dot(g_ref[...], wglu_ref[...], preferred_element_type=F32)
    ssm_out = (glu[:, :D_SSM] * jax.nn.sigmoid(glu[:, D_SSM:])).astype(BF16)
    a = jnp.dot(attn_ref[...], wa_ref[...], preferred_element_type=F32)
    s = jnp.dot(ssm_out, ws_ref[...], preferred_element_type=F32)
    merged = jax.nn.sigmoid(ga_ref[...].astype(F32)) * a + jax.nn.sigmoid(gs_ref[...].astype(F32)) * s
    o_ref[...] = merged.astype(BF16)


def _mix(attn, g, proj, w_glu, w_a, w_s):
    n = attn.shape[0]
    full = lambda r, c_: pl.BlockSpec((r, c_), lambda i: (0, 0))
    return pl.pallas_call(
        _mix_kernel,
        grid=(n // MIX_TB,),
        in_specs=[
            pl.BlockSpec((MIX_TB, D_ATTN), lambda i: (i, 0)),
            pl.BlockSpec((MIX_TB, D_SSM), lambda i: (i, 0)),
            pl.BlockSpec((MIX_TB, D_MODEL), lambda i: (i, GATE_A_COL)),
            pl.BlockSpec((MIX_TB, D_MODEL), lambda i: (i, GATE_A_COL + 1)),
            full(D_SSM, 2 * D_SSM), full(D_ATTN, D_MODEL), full(D_SSM, D_MODEL),
        ],
        out_specs=pl.BlockSpec((MIX_TB, D_MODEL), lambda i: (i, 0)),
        out_shape=jax.ShapeDtypeStruct((n, D_MODEL), BF16),
        compiler_params=_cparams(("arbitrary",)),
        name="mix",
    )(attn, g, proj, proj, w_glu, w_a, w_s)


OUT_TB = 256


def _out_kernel(m_ref, x_ref, wout_ref, gffn_ref, wrt_ref, x2_ref, xn_ref, afft_ref):
    x2 = x_ref[...] + jnp.dot(m_ref[...], wout_ref[...], preferred_element_type=F32)
    x2_ref[...] = x2
    xn = x2 * lax.rsqrt(jnp.mean(x2 * x2, axis=-1, keepdims=True) + EPS) * gffn_ref[...]
    xn_ref[...] = xn
    logits = lax.dot_general(wrt_ref[...], xn.astype(BF16), (((1,), (1,)), ((), ())),
                             preferred_element_type=F32)
    ex = jnp.exp(logits - jnp.max(logits, axis=0, keepdims=True))
    afft_ref[...] = ex / jnp.sum(ex, axis=0, keepdims=True)


def _out_proj(merged, x2d, w_out, g_ffn, w_rt):
    n = x2d.shape[0]
    return pl.pallas_call(
        _out_kernel,
        grid=(n // OUT_TB,),
        in_specs=[
            pl.BlockSpec((OUT_TB, D_MODEL), lambda i: (i, 0)),
            pl.BlockSpec((OUT_TB, D_MODEL), lambda i: (i, 0)),
            pl.BlockSpec((D_MODEL, D_MODEL), lambda i: (0, 0)),
            pl.BlockSpec((1, D_MODEL), lambda i: (0, 0)),
            pl.BlockSpec((N_EXPERTS, D_MODEL), lambda i: (0, 0)),
        ],
        out_specs=[
            pl.BlockSpec((OUT_TB, D_MODEL), lambda i: (i, 0)),
            pl.BlockSpec((OUT_TB, D_MODEL), lambda i: (i, 0)),
            pl.BlockSpec((N_EXPERTS, OUT_TB), lambda i: (0, i)),
        ],
        out_shape=[jax.ShapeDtypeStruct((n, D_MODEL), F32),
                   jax.ShapeDtypeStruct((n, D_MODEL), F32),
                   jax.ShapeDtypeStruct((N_EXPERTS, n), F32)],
        compiler_params=_cparams(("arbitrary",)),
        name="out_proj",
    )(merged, x2d, w_out, g_ffn, w_rt)


F32_INF_BITS = 0x7F800000


def _route_kernel(aff_ref, idx_ref, gate_ref, *, cap):
    aff = aff_ref[0]
    nb = aff.shape[0]
    bits = pltpu.bitcast(aff, jnp.int32)
    tok = (lax.broadcasted_iota(jnp.int32, (nb, LANES), 0) * LANES
           + lax.broadcasted_iota(jnp.int32, (nb, LANES), 1))

    def count(mask):
        return jnp.sum(jnp.where(mask, 1.0, 0.0), keepdims=True)

    def vstep(_, lohi):
        lo, hi = lohi
        mid = lo + ((hi - lo + 1) >> 1)
        ok = count(bits >= mid) >= cap
        return jnp.where(ok, mid, lo), jnp.where(ok, hi, mid - 1)

    thr, _ = lax.fori_loop(0, 32, vstep, (jnp.zeros((1, 1), jnp.int32),
                                          jnp.full((1, 1), F32_INF_BITS, jnp.int32)))
    above = bits > thr
    tied = bits == thr
    need = cap - count(above)

    def tstep(_, lohi):
        lo, hi = lohi
        mid = lo + ((hi - lo) >> 1)
        ok = count(tied & (tok <= mid)) >= need
        return jnp.where(ok, lo, mid + 1), jnp.where(ok, mid, hi)

    cut, _ = lax.fori_loop(0, 32, tstep, (jnp.zeros((1, 1), jnp.int32),
                                          jnp.full((1, 1), nb * LANES - 1, jnp.int32)))
    sel = jnp.where(above | (tied & (tok <= cut)), 1.0, 0.0).astype(BF16)

    r_i = lax.broadcasted_iota(jnp.int32, (LANES, LANES), 0)
    c_i = lax.broadcasted_iota(jnp.int32, (LANES, LANES), 1)
    upper = jnp.where(r_i <= c_i, 1.0, 0.0).astype(BF16)
    cw = jnp.dot(sel, upper, preferred_element_type=F32)
    cnt_row = lax.dot_general(jnp.ones((8, LANES), BF16), sel, (((1,), (1,)), ((), ())),
                              preferred_element_type=F32)
    rb = lax.broadcasted_iota(jnp.int32, (nb, nb), 0)
    cb = lax.broadcasted_iota(jnp.int32, (nb, nb), 1)
    upper_nb = jnp.where(rb <= cb, 1.0, 0.0).astype(BF16)
    incl_row = jnp.dot(cnt_row.astype(BF16), upper_nb, preferred_element_type=F32)[0:1]
    excl_row = incl_row - cnt_row[0:1]

    slot = lax.broadcasted_iota(jnp.int32, (cap, 1), 0).astype(F32)
    blk = jnp.sum(jnp.where(incl_row <= slot, 1.0, 0.0), axis=1, keepdims=True)
    onehot_b = lax.broadcasted_iota(jnp.int32, (cap, nb), 1).astype(F32) == blk
    base = jnp.sum(jnp.where(onehot_b, excl_row, 0.0), axis=1, keepdims=True)
    oh = jnp.where(onehot_b, 1.0, 0.0).astype(BF16)
    cw_j = jnp.dot(oh, cw.astype(BF16), preferred_element_type=F32)
    loc = jnp.sum(jnp.where(cw_j <= slot - base, 1.0, 0.0), axis=1, keepdims=True)
    idx = (blk * LANES + loc).astype(jnp.int32)
    idx_ref[0] = jnp.broadcast_to(idx, (cap, LANES))

    a_hi = aff.astype(BF16)
    r1 = aff - a_hi.astype(F32)
    a_mid = r1.astype(BF16)
    a_lo = (r1 - a_mid.astype(F32)).astype(BF16)
    aff_j = (jnp.dot(oh, a_hi, preferred_element_type=F32) + jnp.dot(oh, a_mid, preferred_element_type=F32)
             + jnp.dot(oh, a_lo, preferred_element_type=F32))
    lane = lax.broadcasted_iota(jnp.int32, (cap, LANES), 1).astype(F32)
    gate = jnp.sum(jnp.where(lane == loc, aff_j, 0.0), axis=1, keepdims=True)
    gate_ref[0] = jnp.broadcast_to(gate, (cap, LANES))


def _route(aff3, cap):
    e, nb, _ = aff3.shape
    return pl.pallas_call(
        functools.partial(_route_kernel, cap=cap),
        grid=(e,),
        in_specs=[pl.BlockSpec((1, nb, LANES), lambda i: (i, 0, 0))],
        out_specs=[pl.BlockSpec((1, cap, LANES), lambda i: (i, 0, 0)),
                   pl.BlockSpec((1, cap, LANES), lambda i: (i, 0, 0))],
        out_shape=[jax.ShapeDtypeStruct((e, cap, LANES), jnp.int32),
                   jax.ShapeDtypeStruct((e, cap, LANES), F32)],
        compiler_params=_cparams(("arbitrary",)),
        name="route",
    )(aff3)


FFN_TM = 256


def _ffn_kernel(idxp_ref, idxc_ref, idxn_ref, gate_ref, xn_hbm, acc_in_hbm, wg_hbm, wu_hbm, wd_hbm, acc_hbm,
                w_scr, xbuf, abuf, obuf, xbf, sem_w, sem_x, sem_a, sem_o, *, tm, nt, ne):
    del acc_in_hbm
    e, r = pl.program_id(0), pl.program_id(1)
    s = e * nt + r
    slot = s % 2
    other = 1 - slot
    last_step = ne * nt - 1

    def x_row(idx_ref, j, sl):
        return pltpu.make_async_copy(xn_hbm.at[pl.ds(idx_ref[0, 0, 0, j], 1), :],
                                     xbuf.at[sl, pl.ds(j, 1), :], sem_x.at[sl])

    def a_row(idx_ref, j, sl):
        return pltpu.make_async_copy(acc_hbm.at[pl.ds(idx_ref[0, 0, 0, j], 1), :],
                                     abuf.at[sl, pl.ds(j, 1), :], sem_a.at[sl])

    def o_row(idx_ref, j, sl):
        return pltpu.make_async_copy(obuf.at[sl, pl.ds(j, 1), :],
                                     acc_hbm.at[pl.ds(idx_ref[0, 0, 0, j], 1), :], sem_o.at[sl])

    def wait_tile(buf, sem, sl):
        pltpu.make_async_copy(buf.at[sl], buf.at[sl], sem.at[sl]).wait()

    @pl.when(r == 0)
    def _():
        copies = [pltpu.make_async_copy(w.at[e], w_scr.at[k], sem_w.at[k])
                  for k, w in enumerate((wg_hbm, wu_hbm, wd_hbm))]
        for cp in copies:
            cp.start()
        for cp in copies:
            cp.wait()

    @pl.when(s == 0)
    def _():
        def first(j, c_):
            x_row(idxc_ref, j, 0).start()
            a_row(idxc_ref, j, 0).start()
            return c_
        lax.fori_loop(0, tm, first, 0)

    wait_tile(xbuf, sem_x, slot)

    @pl.when((r >= 2) | ((r == 0) & (s >= 2)))
    def _():
        wait_tile(obuf, sem_o, slot)

    xbf[...] = xbuf[slot].astype(BF16)

    def compute():
        x = xbf[...]
        hid = (jax.nn.silu(jnp.dot(x, w_scr[0], preferred_element_type=F32))
               * jnp.dot(x, w_scr[1], preferred_element_type=F32))
        y = jnp.dot(hid.astype(BF16), w_scr[2], preferred_element_type=F32)
        obuf[slot] = y * gate_ref[0][:, 0:1]

    def start_writeback_prev():
        for j in range(tm):
            o_row(idxp_ref, j, other).start()

    def start_gather_next(with_acc):
        for j in range(tm):
            x_row(idxn_ref, j, other).start()
            if with_acc:
                a_row(idxn_ref, j, other).start()

    @pl.when(s == 0)
    def _():
        start_gather_next(True)
        compute()

    @pl.when((s > 0) & (r < nt - 1))
    def _():
        start_writeback_prev()
        start_gather_next(True)
        compute()

    @pl.when(r == nt - 1)
    def _():
        start_writeback_prev()
        start_gather_next(False)
        compute()

    @pl.when((r == 0) & (s > 0))
    def _():
        wait_tile(obuf, sem_o, other)
        def gather(j, c_):
            a_row(idxc_ref, j, slot).start()
            return c_
        lax.fori_loop(0, tm, gather, 0)

    wait_tile(abuf, sem_a, slot)
    obuf[slot] = obuf[slot] + abuf[slot]

    @pl.when(s == last_step)
    def _():
        def writeback(j, c_):
            o_row(idxc_ref, j, slot).start()
            return c_
        lax.fori_loop(0, tm, writeback, 0)
        wait_tile(obuf, sem_o, slot)
        wait_tile(obuf, sem_o, other)
        wait_tile(xbuf, sem_x, other)


def _ffn(idx4, gate, xn, acc, w_gate, w_up, w_down):
    e, nt, _, tm = idx4.shape
    assert nt >= 2
    anyspec = pl.BlockSpec(memory_space=pl.ANY)

    def prev(i, r):
        s = jnp.maximum(i * nt + r - 1, 0)
        return (s // nt, s % nt, 0, 0)

    def nxt(i, r):
        s = jnp.minimum(i * nt + r + 1, e * nt - 1)
        return (s // nt, s % nt, 0, 0)

    idx_spec = lambda f: pl.BlockSpec((1, 1, 1, tm), f, memory_space=pltpu.SMEM)
    return pl.pallas_call(
        functools.partial(_ffn_kernel, tm=tm, nt=nt, ne=e),
        grid=(e, nt),
        in_specs=[
            idx_spec(prev), idx_spec(lambda i, r: (i, r, 0, 0)), idx_spec(nxt),
            pl.BlockSpec((1, tm, LANES), lambda i, r: (i, r, 0)),
            anyspec, anyspec, anyspec, anyspec, anyspec,
        ],
        out_specs=anyspec,
        out_shape=jax.ShapeDtypeStruct(acc.shape, F32),
        scratch_shapes=[pltpu.VMEM((3, D_MODEL, D_MODEL), BF16),
                        pltpu.VMEM((2, tm, D_MODEL), F32),
                        pltpu.VMEM((2, tm, D_MODEL), F32),
                        pltpu.VMEM((2, tm, D_MODEL), F32),
                        pltpu.VMEM((tm, D_MODEL), BF16),
                        pltpu.SemaphoreType.DMA((3,)),
                        pltpu.SemaphoreType.DMA((2,)),
                        pltpu.SemaphoreType.DMA((2,)),
                        pltpu.SemaphoreType.DMA((2,))],
        input_output_aliases={5: 0},
        compiler_params=_cparams(("arbitrary", "arbitrary")),
        name="ffn",
    )(idx4, idx4, idx4, gate, xn, acc, w_gate, w_up, w_down)


def _prep_weights(g_mix, w_in, g_q, g_k, attn_sink, lam_re, lam_im, log_step, b_re, b_im, c_re, c_im,
                  d_skip, w_glu, w_br_attn, w_br_ssm, w_out, g_ffn, w_router, w_gate, w_up, w_down):
    u0 = D_ATTN + 2 * D_KV
    w_main = jnp.concatenate([w_in[:, :D_ATTN], w_in[:, u0 + D_SSM:], w_in[:, D_ATTN:u0]], axis=1).astype(BF16)
    w_ut = w_in[:, u0:u0 + D_SSM].T.astype(BF16)
    g_ = N_SSM_GROUPS
    ls = jnp.broadcast_to(log_step[:, :, None], (2, g_, P))
    zeros = jnp.zeros((g_, P), F32)
    prow = jnp.stack([lam_re[0], lam_im[0], ls[0], lam_re[1], lam_im[1], ls[1], zeros, zeros], axis=1)
    pcol = jnp.swapaxes(prow, 1, 2)
    tr = lambda a: jnp.swapaxes(a, 1, 2)
    bt = jnp.concatenate([tr(b_re[0]), tr(b_im[0]), tr(b_re[1]), tr(b_im[1])], axis=1)
    c = jnp.concatenate([c_re[0], c_im[0], c_re[1], c_im[1]], axis=1)
    ct = jnp.swapaxes(c, 1, 2)
    dvec = jnp.repeat(d_skip.reshape(g_, H), CHUNK, axis=1).reshape(g_, 1, D_FLAT)
    return dict(
        g_mix=g_mix.reshape(1, D_MODEL), w_main=w_main, w_ut=w_ut,
        g_q=g_q.reshape(1, HEAD_DIM), g_k=g_k.reshape(1, HEAD_DIM), sink=attn_sink,
        prow=prow, pcol=pcol, bt=bt, c=c, ct=ct, dvec=dvec,
        w_glu=w_glu.astype(BF16), w_a=w_br_attn.astype(BF16), w_s=w_br_ssm.astype(BF16),
        w_out=w_out.astype(BF16), g_ffn=g_ffn.reshape(1, D_MODEL), w_rt=w_router.T.astype(BF16),
        w_gate=w_gate.astype(BF16), w_up=w_up.astype(BF16), w_down=w_down.astype(BF16))


def _layer(x, w):
    b, s, _ = x.shape
    n = b * s
    x2d = x.reshape(n, D_MODEL)
    proj, ut = _in_proj(x2d, w["g_mix"], w["w_main"], w["w_ut"])
    attn = _attention(proj, w["g_q"], w["g_k"], w["sink"], s)
    u4 = ut.reshape(N_SSM_GROUPS, H, n // CHUNK, CHUNK)
    g4 = _ssm(u4, w["prow"], w["pcol"], w["bt"], w["c"], w["ct"], w["dvec"], s)
    g = g4.reshape(D_SSM, n).T
    merged = _mix(attn, g, proj, w["w_glu"], w["w_a"], w["w_s"])
    x2, xn, afft = _out_proj(merged, x2d, w["w_out"], w["g_ffn"], w["w_rt"])
    cap = max(1, min(n, EC_CAPACITY_FACTOR * n // N_EXPERTS))
    idx, gate = _route(afft.reshape(N_EXPERTS, n // LANES, LANES), cap)
    tm = min(FFN_TM, cap // 2)
    idx4 = idx[:, :, 0].reshape(N_EXPERTS, cap // tm, 1, tm)
    y = _ffn(idx4, gate, xn, x2, w["w_gate"], w["w_up"], w["w_down"])
    return y.reshape(b, s, D_MODEL)


def kernel(x_prompt, x_sample, g_mix, w_in, g_q, g_k, attn_sink, lam_re, lam_im, log_step, b_re, b_im,
           c_re, c_im, d_skip, w_glu, w_br_attn, w_br_ssm, w_out, g_ffn, w_router, w_gate, w_up, w_down):
    depth = g_mix.shape[0]
    y_prompt, y_sample = x_prompt, x_sample
    for l in range(depth):
        w = _prep_weights(g_mix[l], w_in[l], g_q[l], g_k[l], attn_sink[l], lam_re[l], lam_im[l],
                          log_step[l], b_re[l], b_im[l], c_re[l], c_im[l], d_skip[l], w_glu[l],
                          w_br_attn[l], w_br_ssm[l], w_out[l], g_ffn[l], w_router[l], w_gate[l],
                          w_up[l], w_down[l])
        y_prompt = _layer(y_prompt, w)
        y_sample = _layer(y_sample, w)
    return (y_prompt, y_sample)
```

```python
import functools
import math

import jax
import jax.numpy as jnp
from jax import lax
from jax.experimental import pallas as pl
from jax.experimental.pallas import tpu as pltpu

F32 = jnp.float32
BF16 = jnp.bfloat16

D_MODEL = 2048
N_Q_HEADS = 16
N_KV_HEADS = 4
HEAD_DIM = 128
Q_GROUP = N_Q_HEADS // N_KV_HEADS
D_ATTN = N_Q_HEADS * HEAD_DIM
D_KV = N_KV_HEADS * HEAD_DIM
WINDOW = 128
BLOCK = 128
D_SSM = 1024
SSM_GROUP = 16
N_SSM_GROUPS = 64
SSM_STATE = 64
N_EXPERTS = 16
EC_CAPACITY_FACTOR = 2
EPS = 1e-6
NEG_INF = -1e30

CHUNK = 128
LANES = 128
D_PROJ = D_ATTN + 2 * D_KV + 2 * D_MODEL
GATE_A_COL = D_ATTN // D_MODEL
K_COL = D_ATTN + 2 * D_MODEL
VMEM_LIMIT_BYTES = 56 * 1024 * 1024


def _cparams(sem):
    return pltpu.CompilerParams(dimension_semantics=sem, vmem_limit_bytes=VMEM_LIMIT_BYTES)


IN_TB = 512
IN_TN = 1024
IN_NJ = D_PROJ // IN_TN


def _inproj_kernel(x_ref, g_ref, w_ref, wut_ref, proj_ref, ut_ref, xn_scr):
    j = pl.program_id(1)

    @pl.when(j == 0)
    def _():
        x = x_ref[...]
        ms = jnp.mean(x * x, axis=-1, keepdims=True)
        xn_scr[...] = (x * lax.rsqrt(ms + EPS) * g_ref[...]).astype(BF16)

    @pl.when(j < IN_NJ)
    def _():
        proj_ref[...] = jnp.dot(xn_scr[...], w_ref[...], preferred_element_type=F32).astype(BF16)

    @pl.when(j == IN_NJ)
    def _():
        ut_ref[...] = lax.dot_general(wut_ref[...], xn_scr[...], (((1,), (1,)), ((), ())),
                                      preferred_element_type=F32).astype(BF16)


def _in_proj(x2d, g_mix, w_main, w_ut):
    n = x2d.shape[0]
    last = IN_NJ - 1
    return pl.pallas_call(
        _inproj_kernel,
        grid=(n // IN_TB, IN_NJ + 1),
        in_specs=[
            pl.BlockSpec((IN_TB, D_MODEL), lambda i, j: (i, 0)),
            pl.BlockSpec((1, D_MODEL), lambda i, j: (0, 0)),
            pl.BlockSpec((D_MODEL, IN_TN), lambda i, j: (0, jnp.minimum(j, last))),
            pl.BlockSpec((D_SSM, D_MODEL), lambda i, j: (0, 0)),
        ],
        out_specs=[
            pl.BlockSpec((IN_TB, IN_TN), lambda i, j: (i, jnp.minimum(j, last))),
            pl.BlockSpec((D_SSM, IN_TB), lambda i, j: (0, i)),
        ],
        out_shape=[jax.ShapeDtypeStruct((n, D_PROJ), BF16),
                   jax.ShapeDtypeStruct((D_SSM, n), BF16)],
        scratch_shapes=[pltpu.VMEM((IN_TB, D_MODEL), BF16)],
        compiler_params=_cparams(("arbitrary", "arbitrary")),
        name="in_proj",
    )(x2d, g_mix, w_main, w_ut)


KEYS = 3 * BLOCK
ALIBI_SLOPES = tuple(2.0 ** (-8.0 * (h + 1) / N_Q_HEADS) for h in range(N_Q_HEADS))


def _attn_kernel(sink_ref, q_ref, kp_ref, kc_ref, kn_ref, vp_ref, vc_ref, vn_ref, gq_ref, gk_ref,
                 o_ref, *, nb_seq):
    pos = pl.program_id(0) % nb_seq
    k_lo = jnp.where(pos == 0, BLOCK, 0)
    k_hi = jnp.where(pos == nb_seq - 1, 2 * BLOCK, KEYS)
    rows = Q_GROUP * BLOCK
    qi = lax.broadcasted_iota(jnp.int32, (rows, KEYS), 0) % BLOCK
    kj = lax.broadcasted_iota(jnp.int32, (rows, KEYS), 1)
    dist = jnp.abs(qi - kj + BLOCK)
    valid = (dist <= WINDOW) & (kj >= k_lo) & (kj < k_hi)
    distf = dist.astype(F32)
    head_of_row = lax.broadcasted_iota(jnp.int32, (rows, 1), 0) // BLOCK
    gq = gq_ref[...]
    gk = gk_ref[...]

    for kh in range(N_KV_HEADS):
        cs = slice(kh * HEAD_DIM, (kh + 1) * HEAD_DIM)
        k = jnp.concatenate([kp_ref[:, cs], kc_ref[:, cs], kn_ref[:, cs]], axis=0).astype(F32)
        kn = (k * lax.rsqrt(jnp.mean(k * k, axis=-1, keepdims=True) + EPS) * gk).astype(BF16)
        v = jnp.concatenate([vp_ref[:, cs], vc_ref[:, cs], vn_ref[:, cs]], axis=0)
        qs = []
        slope = jnp.zeros((rows, 1), F32)
        sink = jnp.zeros((rows, 1), F32)
        for j in range(Q_GROUP):
            h = kh * Q_GROUP + j
            q = q_ref[:, h * HEAD_DIM:(h + 1) * HEAD_DIM].astype(F32)
            qn = q * lax.rsqrt(jnp.mean(q * q, axis=-1, keepdims=True) + EPS) * gq * (HEAD_DIM ** -0.5)
            qs.append(qn.astype(BF16))
            slope = jnp.where(head_of_row == j, ALIBI_SLOPES[h], slope)
            sink = jnp.where(head_of_row == j, sink_ref[h], sink)
        qg = jnp.concatenate(qs, axis=0)
        s = lax.dot_general(qg, kn, (((1,), (1,)), ((), ())), preferred_element_type=F32)
        s = jnp.where(valid, s - slope * distf, NEG_INF)
        m = jnp.maximum(jnp.max(s, axis=-1, keepdims=True), sink)
        p = jnp.exp(s - m)
        den = jnp.sum(p, axis=-1, keepdims=True) + jnp.exp(sink - m)
        o = jnp.dot(p.astype(BF16), v, preferred_element_type=F32) / den
        for j in range(Q_GROUP):
            h = kh * Q_GROUP + j
            o_ref[:, h * HEAD_DIM:(h + 1) * HEAD_DIM] = o[j * BLOCK:(j + 1) * BLOCK].astype(BF16)


def _attention(proj, g_q, g_k, sink, seq):
    n = proj.shape[0]
    nb_seq = seq // BLOCK
    kcol = K_COL // D_KV
    vcol = kcol + 1

    def prev(i):
        return jnp.where(i % nb_seq == 0, i, i - 1)

    def nxt(i):
        return jnp.where(i % nb_seq == nb_seq - 1, i, i + 1)

    kv = lambda col, f: pl.BlockSpec((BLOCK, D_KV), lambda i: (f(i), col))
    same = lambda i: i
    return pl.pallas_call(
        functools.partial(_attn_kernel, nb_seq=nb_seq),
        grid=(n // BLOCK,),
        in_specs=[
            pl.BlockSpec(memory_space=pltpu.SMEM),
            pl.BlockSpec((BLOCK, D_ATTN), lambda i: (i, 0)),
            kv(kcol, prev), kv(kcol, same), kv(kcol, nxt),
            kv(vcol, prev), kv(vcol, same), kv(vcol, nxt),
            pl.BlockSpec((1, HEAD_DIM), lambda i: (0, 0)),
            pl.BlockSpec((1, HEAD_DIM), lambda i: (0, 0)),
        ],
        out_specs=pl.BlockSpec((BLOCK, D_ATTN), lambda i: (i, 0)),
        out_shape=jax.ShapeDtypeStruct((n, D_ATTN), BF16),
        compiler_params=_cparams(("arbitrary",)),
        name="attention",
    )(sink, proj, proj, proj, proj, proj, proj, proj, g_q, g_k)


H = SSM_GROUP
P = SSM_STATE
D_FLAT = H * CHUNK


def _cexp(tr, ti, k):
    mag = jnp.exp(tr * k)
    return mag * jnp.cos(ti * k), mag * jnp.sin(ti * k)


def _ssm_kernel(*refs, ncs, lseqs):
    ng = len(ncs)
    x_refs = refs[:ng]
    prow_ref, pcol_ref, bt_ref, c_ref, ct_ref, dvec_ref = refs[ng:ng + 6]
    o_refs = refs[ng + 6:2 * ng + 6]
    kk_scr, m_scr, f_scr, e_scr = refs[2 * ng + 6:]
    nc = sum(ncs)
    prow = prow_ref[0]
    pcol = pcol_ref[0]
    bt = bt_ref[0]
    c = c_ref[0]
    ct = ct_ref[0]
    lag = lax.broadcasted_iota(jnp.int32, (1, CHUNK), 1).astype(F32)
    tpos = lax.broadcasted_iota(jnp.int32, (CHUNK, 1), 0).astype(F32)

    theta_rows = []
    for d in range(2):
        lr, li = prow[3 * d:3 * d + 1], prow[3 * d + 1:3 * d + 2]
        st = jnp.exp(prow[3 * d + 2:3 * d + 3])
        tr, ti = lr * st, li * st
        theta_rows.append((tr, ti))
        ar, ai = _cexp(tr, ti, 1.0)
        nr, ni = ar - 1.0, ai
        den = lr * lr + li * li
        cr, ci = (nr * lr + ni * li) / den, (ni * lr - nr * li) / den
        b_re, b_im = bt[2 * H * d:2 * H * d + H], bt[2 * H * d + H:2 * H * (d + 1)]
        bb_re, bb_im = cr * b_re - ci * b_im, cr * b_im + ci * b_re

        lrc, lic = pcol[:, 3 * d:3 * d + 1], pcol[:, 3 * d + 1:3 * d + 2]
        stc = jnp.exp(pcol[:, 3 * d + 2:3 * d + 3])
        trc, tic = lrc * stc, lic * stc

        cb_re, cb_im = [], []
        for ho in range(H):
            c_re, c_im = c[2 * H * d + ho:2 * H * d + ho + 1], c[2 * H * d + H + ho:2 * H * d + H + ho + 1]
            cb_re.append(c_re * bb_re - c_im * bb_im)
            cb_im.append(c_re * bb_im + c_im * bb_re)
        cb_re, cb_im = jnp.concatenate(cb_re, axis=0), jnp.concatenate(cb_im, axis=0)

        expo = lag if d == 0 else (CHUNK - lag)
        vr, vi = _cexp(trc, tic, expo)
        kt = jnp.dot(jnp.concatenate([cb_re, cb_im], axis=1), jnp.concatenate([vr, -vi], axis=0),
                     preferred_element_type=F32, precision=lax.Precision.HIGHEST)
        if d == 0:
            kk_scr[:, CHUNK:] = kt
        else:
            kk_scr[:, :CHUNK] = kt
            k0 = jnp.sum(cb_re, axis=1, keepdims=True)
            kk_scr[:, CHUNK:] = kk_scr[:, CHUNK:] + jnp.where(lag == 0.0, k0, 0.0)

        vsr, vsi = _cexp(tr, ti, (CHUNK - 1.0 - tpos) if d == 0 else tpos)
        for hi in range(H):
            fr = vsr * bb_re[hi:hi + 1] - vsi * bb_im[hi:hi + 1]
            fi = vsr * bb_im[hi:hi + 1] + vsi * bb_re[hi:hi + 1]
            f_scr[hi * CHUNK:(hi + 1) * CHUNK, 2 * P * d:2 * P * (d + 1)] = (
                jnp.concatenate([fr, fi], axis=1).astype(BF16))

        er, ei = _cexp(trc, tic, (lag + 1.0) if d == 0 else (CHUNK - lag))
        for ho in range(H):
            cc_re = ct[:, 2 * H * d + ho:2 * H * d + ho + 1]
            cc_im = ct[:, 2 * H * d + H + ho:2 * H * d + H + ho + 1]
            wr, wi = cc_re * er - cc_im * ei, cc_re * ei + cc_im * er
            e_scr[2 * P * d:2 * P * (d + 1), ho * CHUNK:(ho + 1) * CHUNK] = (
                jnp.concatenate([wr, -wi], axis=0).astype(BF16))

    def build(hi, carry):
        for ho in range(H):
            row = kk_scr[pl.ds(ho * H + hi, 1), :]
            rolled = pltpu.roll(jnp.broadcast_to(row, (CHUNK, 2 * CHUNK)), 0, 1, stride=1, stride_axis=0)
            m_scr[pl.ds(pl.multiple_of(hi * CHUNK, CHUNK), CHUNK), ho * CHUNK:(ho + 1) * CHUNK] = (
                rolled[:, CHUNK:].astype(BF16))
        return carry

    lax.fori_loop(0, H, build, 0)

    x = jnp.concatenate([jnp.concatenate([xr[0, hi] for hi in range(H)], axis=1) for xr in x_refs],
                        axis=0)
    summ = jnp.dot(x, f_scr[...], preferred_element_type=F32)
    row = lax.broadcasted_iota(jnp.int32, (nc, 1), 0)
    pos = jnp.zeros((nc, 1), jnp.int32)
    lseq = jnp.zeros((nc, 1), jnp.int32)
    base = 0
    for n_g, l_g in zip(ncs, lseqs):
        mine = (row >= base) & (row < base + n_g)
        pos = jnp.where(mine, (row - base) % l_g, pos)
        lseq = jnp.where(mine, l_g, lseq)
        base += n_g

    def cmul_rows(tr, ti, k, s):
        ar, ai = _cexp(tr, ti, k)
        a_dup = jnp.concatenate([ar, ar], axis=1)
        a_sgn = jnp.concatenate([-ai, ai], axis=1)
        return a_dup * s + a_sgn * pltpu.roll(s, P, 1)

    carries = []
    for d in range(2):
        tr, ti = theta_rows[d]
        hs = summ[:, 2 * P * d:2 * P * (d + 1)]
        span = 1
        while span < max(lseqs):
            if d == 0:
                sh = jnp.where(pos >= span, pltpu.roll(hs, span, 0), 0.0)
            else:
                sh = jnp.where(pos + span < lseq, pltpu.roll(hs, nc - span, 0), 0.0)
            hs = hs + cmul_rows(tr, ti, float(CHUNK * span), sh)
            span *= 2
        if d == 0:
            carries.append(jnp.where(pos >= 1, pltpu.roll(hs, 1, 0), 0.0))
        else:
            carries.append(jnp.where(pos + 1 < lseq, pltpu.roll(hs, nc - 1, 0), 0.0))

    y = jnp.dot(x, m_scr[...], preferred_element_type=F32)
    y = y + jnp.dot(jnp.concatenate(carries, axis=1).astype(BF16), e_scr[...], preferred_element_type=F32)
    y = y + dvec_ref[0] * x.astype(F32)
    y = jax.nn.gelu(y)
    base = 0
    for o_ref, n_g in zip(o_refs, ncs):
        for ho in range(H):
            o_ref[0, ho] = y[base:base + n_g, ho * CHUNK:(ho + 1) * CHUNK].astype(BF16)
        base += n_g


def _ssm(u4s, prow, pcol, bt, c, ct, dvec, seqs):
    g = u4s[0].shape[0]
    ncs = tuple(u.shape[2] for u in u4s)
    lseqs = tuple(sq // CHUNK for sq in seqs)
    per_g = lambda *blk: pl.BlockSpec((1,) + blk, lambda i: (i,) + (0,) * len(blk))
    return pl.pallas_call(
        functools.partial(_ssm_kernel, ncs=ncs, lseqs=lseqs),
        grid=(g,),
        in_specs=[per_g(H, n_g, CHUNK) for n_g in ncs] + [
            per_g(8, P), per_g(P, 8), per_g(4 * H, P), per_g(4 * H, P), per_g(P, 4 * H), per_g(1, D_FLAT)],
        out_specs=[per_g(H, n_g, CHUNK) for n_g in ncs],
        out_shape=[jax.ShapeDtypeStruct(u.shape, BF16) for u in u4s],
        scratch_shapes=[pltpu.VMEM((H * H, 2 * CHUNK), F32),
                        pltpu.VMEM((D_FLAT, D_FLAT), BF16),
                        pltpu.VMEM((D_FLAT, 4 * P), BF16),
                        pltpu.VMEM((4 * P, D_FLAT), BF16)],
        compiler_params=_cparams(("arbitrary",)),
        name="ssm",
    )(*u4s, prow, pcol, bt, c, ct, dvec)


MIX_TB = 256


def _mix_kernel(attn_ref, g_ref, ga_ref, gs_ref, wglu_ref, wa_ref, ws_ref, o_ref):
    glu = jnp.dot(g_ref[...], wglu_ref[...], preferred_element_type=F32)
    ssm_out = (glu[:, :D_SSM] * jax.nn.sigmoid(glu[:, D_SSM:])).astype(BF16)
    a = jnp.dot(attn_ref[...], wa_ref[...], preferred_element_type=F32)
    s = jnp.dot(ssm_out, ws_ref[...], preferred_element_type=F32)
    merged = jax.nn.sigmoid(ga_ref[...].astype(F32)) * a + jax.nn.sigmoid(gs_ref[...].astype(F32)) * s
    o_ref[...] = merged.astype(BF16)


def _mix(attn, g, proj, w_glu, w_a, w_s):
    n = attn.shape[0]
    full = lambda r, c_: pl.BlockSpec((r, c_), lambda i: (0, 0))
    return pl.pallas_call(
        _mix_kernel,
        grid=(n // MIX_TB,),
        in_specs=[
            pl.BlockSpec((MIX_TB, D_ATTN), lambda i: (i, 0)),
            pl.BlockSpec((MIX_TB, D_SSM), lambda i: (i, 0)),
            pl.BlockSpec((MIX_TB, D_MODEL), lambda i: (i, GATE_A_COL)),
            pl.BlockSpec((MIX_TB, D_MODEL), lambda i: (i, GATE_A_COL + 1)),
            full(D_SSM, 2 * D_SSM), full(D_ATTN, D_MODEL), full(D_SSM, D_MODEL),
        ],
        out_specs=pl.BlockSpec((MIX_TB, D_MODEL), lambda i: (i, 0)),
        out_shape=jax.ShapeDtypeStruct((n, D_MODEL), BF16),
        compiler_params=_cparams(("arbitrary",)),
        name="mix",
    )(attn, g, proj, proj, w_glu, w_a, w_s)


OUT_TB = 256


def _out_kernel(m_ref, x_ref, wout_ref, gffn_ref, wrt_ref, x2_ref, xn_ref, afft_ref):
    x2 = x_ref[...] + jnp.dot(m_ref[...], wout_ref[...], preferred_element_type=F32)
    x2_ref[...] = x2
    xn = x2 * lax.rsqrt(jnp.mean(x2 * x2, axis=-1, keepdims=True) + EPS) * gffn_ref[...]
    xn_ref[...] = xn
    logits = lax.dot_general(wrt_ref[...], xn.astype(BF16), (((1,), (1,)), ((), ())),
                             preferred_element_type=F32)
    ex = jnp.exp(logits - jnp.max(logits, axis=0, keepdims=True))
    afft_ref[...] = ex / jnp.sum(ex, axis=0, keepdims=True)


def _out_proj(merged, x2d, w_out, g_ffn, w_rt):
    n = x2d.shape[0]
    return pl.pallas_call(
        _out_kernel,
        grid=(n // OUT_TB,),
        in_specs=[
            pl.BlockSpec((OUT_TB, D_MODEL), lambda i: (i, 0)),
            pl.BlockSpec((OUT_TB, D_MODEL), lambda i: (i, 0)),
            pl.BlockSpec((D_MODEL, D_MODEL), lambda i: (0, 0)),
            pl.BlockSpec((1, D_MODEL), lambda i: (0, 0)),
            pl.BlockSpec((N_EXPERTS, D_MODEL), lambda i: (0, 0)),
        ],
        out_specs=[
            pl.BlockSpec((OUT_TB, D_MODEL), lambda i: (i, 0)),
            pl.BlockSpec((OUT_TB, D_MODEL), lambda i: (i, 0)),
            pl.BlockSpec((N_EXPERTS, OUT_TB), lambda i: (0, i)),
        ],
        out_shape=[jax.ShapeDtypeStruct((n, D_MODEL), F32),
                   jax.ShapeDtypeStruct((n, D_MODEL), F32),
                   jax.ShapeDtypeStruct((N_EXPERTS, n), F32)],
        compiler_params=_cparams(("arbitrary",)),
        name="out_proj",
    )(merged, x2d, w_out, g_ffn, w_rt)


F32_INF_BITS = 0x7F800000
VALUE_BISECT_STEPS = 31


def _route_thr_kernel(aff_ref, thr_ref, cut_ref, *, cap, ne):
    nb = aff_ref.shape[0] // ne
    bits = pltpu.bitcast(aff_ref[...], jnp.int32)
    tok = (lax.broadcasted_iota(jnp.int32, (nb, LANES), 0) * LANES
           + lax.broadcasted_iota(jnp.int32, (nb, LANES), 1))
    seg_r = lax.broadcasted_iota(jnp.int32, (ne, ne * nb), 0)
    seg_c = lax.broadcasted_iota(jnp.int32, (ne, ne * nb), 1)
    seg = jnp.where((seg_c >= seg_r * nb) & (seg_c < (seg_r + 1) * nb), 1.0, 0.0).astype(BF16)
    ones = jnp.ones((LANES, LANES), BF16)

    def count(masks):
        m = jnp.concatenate([jnp.where(mk, 1.0, 0.0) for mk in masks], axis=0).astype(BF16)
        rows = jnp.dot(m, ones, preferred_element_type=F32)
        return jnp.dot(seg, rows.astype(BF16), preferred_element_type=F32)

    def per_expert(fn):
        return [fn(bits[e * nb:(e + 1) * nb], e) for e in range(ne)]

    def vstep(_, lohi):
        lo, hi = lohi
        mid = lo + ((hi - lo + 1) >> 1)
        ok = count(per_expert(lambda b, e: b >= mid[e:e + 1])) >= cap
        return jnp.where(ok, mid, lo), jnp.where(ok, hi, mid - 1)

    thr, _ = lax.fori_loop(0, VALUE_BISECT_STEPS, vstep,
                           (jnp.zeros((ne, LANES), jnp.int32), jnp.full((ne, LANES), F32_INF_BITS, jnp.int32)))
    need = cap - count(per_expert(lambda b, e: b > thr[e:e + 1]))

    def tstep(_, lohi):
        lo, hi = lohi
        mid = lo + ((hi - lo) >> 1)
        ok = count(per_expert(lambda b, e: (b == thr[e:e + 1]) & (tok <= mid[e:e + 1]))) >= need
        return jnp.where(ok, lo, mid + 1), jnp.where(ok, mid, hi)

    n_tok_bits = max(1, (nb * LANES - 1).bit_length())
    cut, _ = lax.fori_loop(0, n_tok_bits, tstep,
                           (jnp.zeros((ne, LANES), jnp.int32), jnp.full((ne, LANES), nb * LANES - 1, jnp.int32)))
    thr_ref[...] = thr
    cut_ref[...] = cut


def _route_compact_kernel(aff_ref, thr_ref, cut_ref, idx_ref, gate_ref, *, cap):
    aff = aff_ref[0]
    nb = aff.shape[0]
    bits = pltpu.bitcast(aff, jnp.int32)
    tok = (lax.broadcasted_iota(jnp.int32, (nb, LANES), 0) * LANES
           + lax.broadcasted_iota(jnp.int32, (nb, LANES), 1))
    thr, cut = thr_ref[0], cut_ref[0]
    sel = jnp.where((bits > thr) | ((bits == thr) & (tok <= cut)), 1.0, 0.0).astype(BF16)

    r_i = lax.broadcasted_iota(jnp.int32, (LANES, LANES), 0)
    c_i = lax.broadcasted_iota(jnp.int32, (LANES, LANES), 1)
    upper = jnp.where(r_i <= c_i, 1.0, 0.0).astype(BF16)
    cw = jnp.dot(sel, upper, preferred_element_type=F32)
    cnt_row = lax.dot_general(jnp.ones((8, LANES), BF16), sel, (((1,), (1,)), ((), ())),
                              preferred_element_type=F32)
    rb = lax.broadcasted_iota(jnp.int32, (nb, nb), 0)
    cb = lax.broadcasted_iota(jnp.int32, (nb, nb), 1)
    upper_nb = jnp.where(rb <= cb, 1.0, 0.0).astype(BF16)
    incl_row = jnp.dot(cnt_row.astype(BF16), upper_nb, preferred_element_type=F32)[0:1]
    excl_row = incl_row - cnt_row[0:1]

    slot = lax.broadcasted_iota(jnp.int32, (cap, 1), 0).astype(F32)
    blk = jnp.sum(jnp.where(incl_row <= slot, 1.0, 0.0), axis=1, keepdims=True)
    onehot_b = lax.broadcasted_iota(jnp.int32, (cap, nb), 1).astype(F32) == blk
    base = jnp.sum(jnp.where(onehot_b, excl_row, 0.0), axis=1, keepdims=True)
    oh = jnp.where(onehot_b, 1.0, 0.0).astype(BF16)
    cw_j = jnp.dot(oh, cw.astype(BF16), preferred_element_type=F32)
    loc = jnp.sum(jnp.where(cw_j <= slot - base, 1.0, 0.0), axis=1, keepdims=True)
    idx = (blk * LANES + loc).astype(jnp.int32)
    idx_ref[0] = jnp.broadcast_to(idx, (cap, LANES))

    a_hi = aff.astype(BF16)
    r1 = aff - a_hi.astype(F32)
    a_mid = r1.astype(BF16)
    a_lo = (r1 - a_mid.astype(F32)).astype(BF16)
    aff_j = (jnp.dot(oh, a_hi, preferred_element_type=F32) + jnp.dot(oh, a_mid, preferred_element_type=F32)
             + jnp.dot(oh, a_lo, preferred_element_type=F32))
    lane = lax.broadcasted_iota(jnp.int32, (cap, LANES), 1).astype(F32)
    gate = jnp.sum(jnp.where(lane == loc, aff_j, 0.0), axis=1, keepdims=True)
    gate_ref[0] = jnp.broadcast_to(gate, (cap, LANES))


def _route(aff3, cap):
    e, nb, _ = aff3.shape
    thr, cut = pl.pallas_call(
        functools.partial(_route_thr_kernel, cap=cap, ne=e),
        grid=(1,),
        in_specs=[pl.BlockSpec((e * nb, LANES), lambda i: (0, 0))],
        out_specs=[pl.BlockSpec((e, LANES), lambda i: (0, 0))] * 2,
        out_shape=[jax.ShapeDtypeStruct((e, LANES), jnp.int32)] * 2,
        compiler_params=_cparams(("arbitrary",)),
        name="route_thr",
    )(aff3.reshape(e * nb, LANES))
    row = pl.BlockSpec((1, 1, LANES), lambda i: (i, 0, 0))
    return pl.pallas_call(
        functools.partial(_route_compact_kernel, cap=cap),
        grid=(e,),
        in_specs=[pl.BlockSpec((1, nb, LANES), lambda i: (i, 0, 0)), row, row],
        out_specs=[pl.BlockSpec((1, cap, LANES), lambda i: (i, 0, 0)),
                   pl.BlockSpec((1, cap, LANES), lambda i: (i, 0, 0))],
        out_shape=[jax.ShapeDtypeStruct((e, cap, LANES), jnp.int32),
                   jax.ShapeDtypeStruct((e, cap, LANES), F32)],
        compiler_params=_cparams(("arbitrary",)),
        name="route_compact",
    )(aff3, thr.reshape(e, 1, LANES), cut.reshape(e, 1, LANES))


FFN_TM = 256


def _ffn_kernel(idxp_ref, idxc_ref, idxn_ref, gate_ref, xn_hbm, acc_in_hbm, wg_hbm, wu_hbm, wd_hbm, acc_hbm,
                w_scr, xbuf, abuf, obuf, xbf, sem_w, sem_x, sem_a, sem_o, *, tm, nt, ne):
    del acc_in_hbm
    e, r = pl.program_id(0), pl.program_id(1)
    s = e * nt + r
    slot = s % 2
    other = 1 - slot
    last_step = ne * nt - 1

    def x_row(idx_ref, j, sl):
        return pltpu.make_async_copy(xn_hbm.at[pl.ds(idx_ref[0, 0, 0, j], 1), :],
                                     xbuf.at[sl, pl.ds(j, 1), :], sem_x.at[sl])

    def a_row(idx_ref, j, sl):
        return pltpu.make_async_copy(acc_hbm.at[pl.ds(idx_ref[0, 0, 0, j], 1), :],
                                     abuf.at[sl, pl.ds(j, 1), :], sem_a.at[sl])

    def o_row(idx_ref, j, sl):
        return pltpu.make_async_copy(obuf.at[sl, pl.ds(j, 1), :],
                                     acc_hbm.at[pl.ds(idx_ref[0, 0, 0, j], 1), :], sem_o.at[sl])

    def wait_tile(buf, sem, sl):
        pltpu.make_async_copy(buf.at[sl], buf.at[sl], sem.at[sl]).wait()

    @pl.when(r == 0)
    def _():
        copies = [pltpu.make_async_copy(w.at[e], w_scr.at[k], sem_w.at[k])
                  for k, w in enumerate((wg_hbm, wu_hbm, wd_hbm))]
        for cp in copies:
            cp.start()
        for cp in copies:
            cp.wait()

    @pl.when(s == 0)
    def _():
        def first(j, c_):
            x_row(idxc_ref, j, 0).start()
            a_row(idxc_ref, j, 0).start()
            return c_
        lax.fori_loop(0, tm, first, 0)

    wait_tile(xbuf, sem_x, slot)

    @pl.when((r >= 2) | ((r == 0) & (s >= 2)))
    def _():
        wait_tile(obuf, sem_o, slot)

    xbf[...] = xbuf[slot].astype(BF16)

    def compute():
        x = xbf[...]
        hid = (jax.nn.silu(jnp.dot(x, w_scr[0], preferred_element_type=F32))
               * jnp.dot(x, w_scr[1], preferred_element_type=F32))
        y = jnp.dot(hid.astype(BF16), w_scr[2], preferred_element_type=F32)
        obuf[slot] = y * gate_ref[0][:, 0:1]

    def start_writeback_prev():
        for j in range(tm):
            o_row(idxp_ref, j, other).start()

    def start_gather_next(with_acc):
        for j in range(tm):
            x_row(idxn_ref, j, other).start()
            if with_acc:
                a_row(idxn_ref, j, other).start()

    @pl.when(s == 0)
    def _():
        start_gather_next(True)
        compute()

    @pl.when((s > 0) & (r < nt - 1))
    def _():
        start_writeback_prev()
        start_gather_next(True)
        compute()

    @pl.when(r == nt - 1)
    def _():
        start_writeback_prev()
        start_gather_next(False)
        compute()

    @pl.when((r == 0) & (s > 0))
    def _():
        wait_tile(obuf, sem_o, other)
        def gather(j, c_):
            a_row(idxc_ref, j, slot).start()
            return c_
        lax.fori_loop(0, tm, gather, 0)

    wait_tile(abuf, sem_a, slot)
    obuf[slot] = obuf[slot] + abuf[slot]

    @pl.when(s == last_step)
    def _():
        def writeback(j, c_):
            o_row(idxc_ref, j, slot).start()
            return c_
        lax.fori_loop(0, tm, writeback, 0)
        wait_tile(obuf, sem_o, slot)
        wait_tile(obuf, sem_o, other)
        wait_tile(xbuf, sem_x, other)


def _ffn(idx4, gate, xn, acc, w_gate, w_up, w_down):
    e, nt, _, tm = idx4.shape
    assert nt >= 2
    anyspec = pl.BlockSpec(memory_space=pl.ANY)

    def prev(i, r):
        s = jnp.maximum(i * nt + r - 1, 0)
        return (s // nt, s % nt, 0, 0)

    def nxt(i, r):
        s = jnp.minimum(i * nt + r + 1, e * nt - 1)
        return (s // nt, s % nt, 0, 0)

    idx_spec = lambda f: pl.BlockSpec((1, 1, 1, tm), f, memory_space=pltpu.SMEM)
    return pl.pallas_call(
        functools.partial(_ffn_kernel, tm=tm, nt=nt, ne=e),
        grid=(e, nt),
        in_specs=[
            idx_spec(prev), idx_spec(lambda i, r: (i, r, 0, 0)), idx_spec(nxt),
            pl.BlockSpec((1, tm, LANES), lambda i, r: (i, r, 0)),
            anyspec, anyspec, anyspec, anyspec, anyspec,
        ],
        out_specs=anyspec,
        out_shape=jax.ShapeDtypeStruct(acc.shape, F32),
        scratch_shapes=[pltpu.VMEM((3, D_MODEL, D_MODEL), BF16),
                        pltpu.VMEM((2, tm, D_MODEL), F32),
                        pltpu.VMEM((2, tm, D_MODEL), F32),
                        pltpu.VMEM((2, tm, D_MODEL), F32),
                        pltpu.VMEM((tm, D_MODEL), BF16),
                        pltpu.SemaphoreType.DMA((3,)),
                        pltpu.SemaphoreType.DMA((2,)),
                        pltpu.SemaphoreType.DMA((2,)),
                        pltpu.SemaphoreType.DMA((2,))],
        input_output_aliases={5: 0},
        compiler_params=_cparams(("arbitrary", "arbitrary")),
        name="ffn",
    )(idx4, idx4, idx4, gate, xn, acc, w_gate, w_up, w_down)


def _prep_weights(g_mix, w_in, g_q, g_k, attn_sink, lam_re, lam_im, log_step, b_re, b_im, c_re, c_im,
                  d_skip, w_glu, w_br_attn, w_br_ssm, w_out, g_ffn, w_router, w_gate, w_up, w_down):
    u0 = D_ATTN + 2 * D_KV
    w_main = jnp.concatenate([w_in[:, :D_ATTN], w_in[:, u0 + D_SSM:], w_in[:, D_ATTN:u0]], axis=1).astype(BF16)
    w_ut = w_in[:, u0:u0 + D_SSM].T.astype(BF16)
    g_ = N_SSM_GROUPS
    ls = jnp.broadcast_to(log_step[:, :, None], (2, g_, P))
    zeros = jnp.zeros((g_, P), F32)
    prow = jnp.stack([lam_re[0], lam_im[0], ls[0], lam_re[1], lam_im[1], ls[1], zeros, zeros], axis=1)
    pcol = jnp.swapaxes(prow, 1, 2)
    tr = lambda a: jnp.swapaxes(a, 1, 2)
    bt = jnp.concatenate([tr(b_re[0]), tr(b_im[0]), tr(b_re[1]), tr(b_im[1])], axis=1)
    c = jnp.concatenate([c_re[0], c_im[0], c_re[1], c_im[1]], axis=1)
    ct = jnp.swapaxes(c, 1, 2)
    dvec = jnp.repeat(d_skip.reshape(g_, H), CHUNK, axis=1).reshape(g_, 1, D_FLAT)
    return dict(
        g_mix=g_mix.reshape(1, D_MODEL), w_main=w_main, w_ut=w_ut,
        g_q=g_q.reshape(1, HEAD_DIM), g_k=g_k.reshape(1, HEAD_DIM), sink=attn_sink,
        prow=prow, pcol=pcol, bt=bt, c=c, ct=ct, dvec=dvec,
        w_glu=w_glu.astype(BF16), w_a=w_br_attn.astype(BF16), w_s=w_br_ssm.astype(BF16),
        w_out=w_out.astype(BF16), g_ffn=g_ffn.reshape(1, D_MODEL), w_rt=w_router.T.astype(BF16),
        w_gate=w_gate.astype(BF16), w_up=w_up.astype(BF16), w_down=w_down.astype(BF16))


def _layers(xs, w):
    shapes = [x.shape for x in xs]
    x2ds = [x.reshape(b * s, D_MODEL) for x, (b, s, _) in zip(xs, shapes)]
    pre = [_in_proj(x2d, w["g_mix"], w["w_main"], w["w_ut"]) for x2d in x2ds]
    u4s = [ut.reshape(N_SSM_GROUPS, H, ut.shape[1] // CHUNK, CHUNK) for _, ut in pre]
    g4s = _ssm(u4s, w["prow"], w["pcol"], w["bt"], w["c"], w["ct"], w["dvec"], [s for _, s, _ in shapes])
    outs = []
    for (b, s, _), x2d, (proj, _), g4 in zip(shapes, x2ds, pre, g4s):
        n = b * s
        attn = _attention(proj, w["g_q"], w["g_k"], w["sink"], s)
        g = g4.reshape(D_SSM, n).T
        merged = _mix(attn, g, proj, w["w_glu"], w["w_a"], w["w_s"])
        x2, xn, afft = _out_proj(merged, x2d, w["w_out"], w["g_ffn"], w["w_rt"])
        cap = max(1, min(n, EC_CAPACITY_FACTOR * n // N_EXPERTS))
        idx, gate = _route(afft.reshape(N_EXPERTS, n // LANES, LANES), cap)
        tm = min(FFN_TM, cap // 2)
        idx4 = idx[:, :, 0].reshape(N_EXPERTS, cap // tm, 1, tm)
        y = _ffn(idx4, gate, xn, x2, w["w_gate"], w["w_up"], w["w_down"])
        outs.append(y.reshape(b, s, D_MODEL))
    return outs


def kernel(x_prompt, x_sample, g_mix, w_in, g_q, g_k, attn_sink, lam_re, lam_im, log_step, b_re, b_im,
           c_re, c_im, d_skip, w_glu, w_br_attn, w_br_ssm, w_out, g_ffn, w_router, w_gate, w_up, w_down):
    depth = g_mix.shape[0]
    y_prompt, y_sample = x_prompt, x_sample
    for l in range(depth):
        w = _prep_weights(g_mix[l], w_in[l], g_q[l], g_k[l], attn_sink[l], lam_re[l], lam_im[l],
                          log_step[l], b_re[l], b_im[l], c_re[l], c_im[l], d_skip[l], w_glu[l],
                          w_br_attn[l], w_br_ssm[l], w_out[l], g_ffn[l], w_router[l], w_gate[l],
                          w_up[l], w_down[l])
        y_prompt, y_sample = _layers([y_prompt, y_sample], w)
    return (y_prompt, y_sample)
```

```python
import functools
import math

import jax
import jax.numpy as jnp
from jax import lax
from jax.experimental import pallas as pl
from jax.experimental.pallas import tpu as pltpu

F32 = jnp.float32
BF16 = jnp.bfloat16

D_MODEL = 2048
N_Q_HEADS = 16
N_KV_HEADS = 4
HEAD_DIM = 128
Q_GROUP = N_Q_HEADS // N_KV_HEADS
D_ATTN = N_Q_HEADS * HEAD_DIM
D_KV = N_KV_HEADS * HEAD_DIM
WINDOW = 128
BLOCK = 128
D_SSM = 1024
SSM_GROUP = 16
N_SSM_GROUPS = 64
SSM_STATE = 64
N_EXPERTS = 16
EC_CAPACITY_FACTOR = 2
EPS = 1e-6
NEG_INF = -1e30

CHUNK = 128
LANES = 128
D_PROJ = D_ATTN + 2 * D_KV + 2 * D_MODEL
GATE_A_COL = D_ATTN // D_MODEL
K_COL = D_ATTN + 2 * D_MODEL
VMEM_LIMIT_BYTES = 56 * 1024 * 1024


def _cparams(sem):
    return pltpu.CompilerParams(dimension_semantics=sem, vmem_limit_bytes=VMEM_LIMIT_BYTES)


IN_TB = 1024
IN_TN = 1024
IN_NJ = D_PROJ // IN_TN


def _inproj_kernel(x_ref, g_ref, w_ref, wut_ref, proj_ref, ut_ref, xn_scr):
    j = pl.program_id(1)

    @pl.when(j == 0)
    def _():
        x = x_ref[...]
        ms = jnp.mean(x * x, axis=-1, keepdims=True)
        xn_scr[...] = (x * lax.rsqrt(ms + EPS) * g_ref[...]).astype(BF16)

    @pl.when(j < IN_NJ)
    def _():
        proj_ref[...] = jnp.dot(xn_scr[...], w_ref[...], preferred_element_type=F32).astype(BF16)

    @pl.when(j == IN_NJ)
    def _():
        ut_ref[...] = lax.dot_general(wut_ref[...], xn_scr[...], (((1,), (1,)), ((), ())),
                                      preferred_element_type=F32).astype(BF16)


def _in_proj(x2d, g_mix, w_main, w_ut):
    n = x2d.shape[0]
    last = IN_NJ - 1
    return pl.pallas_call(
        _inproj_kernel,
        grid=(n // IN_TB, IN_NJ + 1),
        in_specs=[
            pl.BlockSpec((IN_TB, D_MODEL), lambda i, j: (i, 0)),
            pl.BlockSpec((1, D_MODEL), lambda i, j: (0, 0)),
            pl.BlockSpec((D_MODEL, IN_TN), lambda i, j: (0, jnp.minimum(j, last))),
            pl.BlockSpec((D_SSM, D_MODEL), lambda i, j: (0, 0), pipeline_mode=pl.Buffered(1)),
        ],
        out_specs=[
            pl.BlockSpec((IN_TB, IN_TN), lambda i, j: (i, jnp.minimum(j, last))),
            pl.BlockSpec((D_SSM, IN_TB), lambda i, j: (0, i)),
        ],
        out_shape=[jax.ShapeDtypeStruct((n, D_PROJ), BF16),
                   jax.ShapeDtypeStruct((D_SSM, n), BF16)],
        scratch_shapes=[pltpu.VMEM((IN_TB, D_MODEL), BF16)],
        compiler_params=_cparams(("arbitrary", "arbitrary")),
        name="in_proj",
    )(x2d, g_mix, w_main, w_ut)


KEYS = 3 * BLOCK
ALIBI_SLOPES = tuple(2.0 ** (-8.0 * (h + 1) / N_Q_HEADS) for h in range(N_Q_HEADS))
LOG2E = math.log2(math.e)


def _attn_kernel(sink_ref, q_ref, kp_ref, kc_ref, kn_ref, vp_ref, vc_ref, vn_ref, gq_ref, gk_ref,
                 o_ref, bias_scr, *, nb_seq):
    rows = Q_GROUP * BLOCK

    @pl.when(pl.program_id(0) == 0)
    def _():
        qi = lax.broadcasted_iota(jnp.int32, (BLOCK, KEYS), 0)
        kj = lax.broadcasted_iota(jnp.int32, (BLOCK, KEYS), 1)
        dist = jnp.abs(qi - kj + BLOCK)
        distf = dist.astype(F32)
        for h in range(N_Q_HEADS):
            bias_scr[h] = jnp.where(dist <= WINDOW, distf * (-ALIBI_SLOPES[h] * LOG2E), NEG_INF)

    pos = pl.program_id(0) % nb_seq
    k_lo = jnp.where(pos == 0, BLOCK, 0)
    k_hi = jnp.where(pos == nb_seq - 1, 2 * BLOCK, KEYS)
    kj1 = lax.broadcasted_iota(jnp.int32, (1, KEYS), 1)
    edge = jnp.where((kj1 >= k_lo) & (kj1 < k_hi), 0.0, NEG_INF)
    head_of_row = lax.broadcasted_iota(jnp.int32, (rows, 1), 0) // BLOCK
    gq = gq_ref[...] * (HEAD_DIM ** -0.5 * LOG2E)
    gk = gk_ref[...]
    ones = jnp.ones((KEYS, HEAD_DIM), BF16)

    for kh in range(N_KV_HEADS):
        cs = slice(kh * HEAD_DIM, (kh + 1) * HEAD_DIM)
        k = jnp.concatenate([kp_ref[:, cs], kc_ref[:, cs], kn_ref[:, cs]], axis=0).astype(F32)
        kn = (k * lax.rsqrt(jnp.mean(k * k, axis=-1, keepdims=True) + EPS) * gk).astype(BF16)
        v1 = jnp.concatenate([jnp.concatenate([vp_ref[:, cs], vc_ref[:, cs], vn_ref[:, cs]], axis=0), ones],
                             axis=1)
        qs = []
        sink = jnp.zeros((rows, 1), F32)
        for j in range(Q_GROUP):
            h = kh * Q_GROUP + j
            q = q_ref[:, h * HEAD_DIM:(h + 1) * HEAD_DIM].astype(F32)
            qs.append((q * lax.rsqrt(jnp.mean(q * q, axis=-1, keepdims=True) + EPS) * gq).astype(BF16))
            sink = jnp.where(head_of_row == j, sink_ref[h] * LOG2E, sink)
        qg = jnp.concatenate(qs, axis=0)
        s = lax.dot_general(qg, kn, (((1,), (1,)), ((), ())), preferred_element_type=F32)
        bias = jnp.concatenate([bias_scr[kh * Q_GROUP + j] for j in range(Q_GROUP)], axis=0)
        s = s + bias + edge
        m = jnp.maximum(jnp.max(s, axis=-1, keepdims=True), sink)
        p = jnp.exp2(s - m).astype(BF16)
        pv = jnp.dot(p, v1, preferred_element_type=F32)
        den = pv[:, HEAD_DIM:HEAD_DIM + 1] + jnp.exp2(sink - m)
        o = pv[:, :HEAD_DIM] / den
        for j in range(Q_GROUP):
            h = kh * Q_GROUP + j
            o_ref[:, h * HEAD_DIM:(h + 1) * HEAD_DIM] = o[j * BLOCK:(j + 1) * BLOCK].astype(BF16)


def _attention(proj, g_q, g_k, sink, seq):
    n = proj.shape[0]
    nb_seq = seq // BLOCK
    kcol = K_COL // D_KV
    vcol = kcol + 1

    def prev(i):
        return jnp.where(i % nb_seq == 0, i, i - 1)

    def nxt(i):
        return jnp.where(i % nb_seq == nb_seq - 1, i, i + 1)

    kv = lambda col, f: pl.BlockSpec((BLOCK, D_KV), lambda i: (f(i), col))
    same = lambda i: i
    return pl.pallas_call(
        functools.partial(_attn_kernel, nb_seq=nb_seq),
        grid=(n // BLOCK,),
        in_specs=[
            pl.BlockSpec(memory_space=pltpu.SMEM),
            pl.BlockSpec((BLOCK, D_ATTN), lambda i: (i, 0)),
            kv(kcol, prev), kv(kcol, same), kv(kcol, nxt),
            kv(vcol, prev), kv(vcol, same), kv(vcol, nxt),
            pl.BlockSpec((1, HEAD_DIM), lambda i: (0, 0)),
            pl.BlockSpec((1, HEAD_DIM), lambda i: (0, 0)),
        ],
        out_specs=pl.BlockSpec((BLOCK, D_ATTN), lambda i: (i, 0)),
        out_shape=jax.ShapeDtypeStruct((n, D_ATTN), BF16),
        scratch_shapes=[pltpu.VMEM((N_Q_HEADS, BLOCK, KEYS), F32)],
        compiler_params=_cparams(("arbitrary",)),
        name="attention",
    )(sink, proj, proj, proj, proj, proj, proj, proj, g_q, g_k)


H = SSM_GROUP
P = SSM_STATE
D_FLAT = H * CHUNK


def _cexp(tr, ti, k):
    mag = jnp.exp(tr * k)
    return mag * jnp.cos(ti * k), mag * jnp.sin(ti * k)


def _ssm_kernel(*refs, ncs, lseqs):
    ng = len(ncs)
    x_refs = refs[:ng]
    prow_ref, pcol_ref, bt_ref, c_ref, ct_ref, dvec_ref = refs[ng:ng + 6]
    o_refs = refs[ng + 6:2 * ng + 6]
    kk_scr, m_scr, f_scr, e_scr = refs[2 * ng + 6:]
    nc = sum(ncs)
    prow = prow_ref[0]
    pcol = pcol_ref[0]
    bt = bt_ref[0]
    c = c_ref[0]
    ct = ct_ref[0]
    lag = lax.broadcasted_iota(jnp.int32, (1, CHUNK), 1).astype(F32)
    tpos = lax.broadcasted_iota(jnp.int32, (CHUNK, 1), 0).astype(F32)

    theta_rows = []
    for d in range(2):
        lr, li = prow[3 * d:3 * d + 1], prow[3 * d + 1:3 * d + 2]
        st = jnp.exp(prow[3 * d + 2:3 * d + 3])
        tr, ti = lr * st, li * st
        theta_rows.append((tr, ti))
        ar, ai = _cexp(tr, ti, 1.0)
        nr, ni = ar - 1.0, ai
        den = lr * lr + li * li
        cr, ci = (nr * lr + ni * li) / den, (ni * lr - nr * li) / den
        b_re, b_im = bt[2 * H * d:2 * H * d + H], bt[2 * H * d + H:2 * H * (d + 1)]
        bb_re, bb_im = cr * b_re - ci * b_im, cr * b_im + ci * b_re

        lrc, lic = pcol[:, 3 * d:3 * d + 1], pcol[:, 3 * d + 1:3 * d + 2]
        stc = jnp.exp(pcol[:, 3 * d + 2:3 * d + 3])
        trc, tic = lrc * stc, lic * stc

        cb_re, cb_im = [], []
        for ho in range(H):
            c_re, c_im = c[2 * H * d + ho:2 * H * d + ho + 1], c[2 * H * d + H + ho:2 * H * d + H + ho + 1]
            cb_re.append(c_re * bb_re - c_im * bb_im)
            cb_im.append(c_re * bb_im + c_im * bb_re)
        cb_re, cb_im = jnp.concatenate(cb_re, axis=0), jnp.concatenate(cb_im, axis=0)

        expo = lag if d == 0 else (CHUNK - lag)
        vr, vi = _cexp(trc, tic, expo)
        kt = jnp.dot(jnp.concatenate([cb_re, cb_im], axis=1), jnp.concatenate([vr, -vi], axis=0),
                     preferred_element_type=F32, precision=lax.Precision.HIGHEST)
        if d == 0:
            kk_scr[:, CHUNK:] = kt
        else:
            kk_scr[:, :CHUNK] = kt
            k0 = jnp.sum(cb_re, axis=1, keepdims=True)
            kk_scr[:, CHUNK:] = kk_scr[:, CHUNK:] + jnp.where(lag == 0.0, k0, 0.0)

        vsr, vsi = _cexp(tr, ti, (CHUNK - 1.0 - tpos) if d == 0 else tpos)
        for hi in range(H):
            fr = vsr * bb_re[hi:hi + 1] - vsi * bb_im[hi:hi + 1]
            fi = vsr * bb_im[hi:hi + 1] + vsi * bb_re[hi:hi + 1]
            f_scr[hi * CHUNK:(hi + 1) * CHUNK, 2 * P * d:2 * P * (d + 1)] = (
                jnp.concatenate([fr, fi], axis=1).astype(BF16))

        er, ei = _cexp(trc, tic, (lag + 1.0) if d == 0 else (CHUNK - lag))
        for ho in range(H):
            cc_re = ct[:, 2 * H * d + ho:2 * H * d + ho + 1]
            cc_im = ct[:, 2 * H * d + H + ho:2 * H * d + H + ho + 1]
            wr, wi = cc_re * er - cc_im * ei, cc_re * ei + cc_im * er
            e_scr[2 * P * d:2 * P * (d + 1), ho * CHUNK:(ho + 1) * CHUNK] = (
                jnp.concatenate([wr, -wi], axis=0).astype(BF16))

    def build(hi, carry):
        for ho in range(H):
            row = kk_scr[pl.ds(ho * H + hi, 1), :]
            rolled = pltpu.roll(jnp.broadcast_to(row, (CHUNK, 2 * CHUNK)), 0, 1, stride=1, stride_axis=0)
            m_scr[pl.ds(pl.multiple_of(hi * CHUNK, CHUNK), CHUNK), ho * CHUNK:(ho + 1) * CHUNK] = (
                rolled[:, CHUNK:].astype(BF16))
        return carry

    lax.fori_loop(0, H, build, 0)

    x = jnp.concatenate([jnp.concatenate([xr[0, hi] for hi in range(H)], axis=1) for xr in x_refs],
                        axis=0)
    summ = jnp.dot(x, f_scr[...], preferred_element_type=F32)
    row = lax.broadcasted_iota(jnp.int32, (nc, 1), 0)
    pos = jnp.zeros((nc, 1), jnp.int32)
    lseq = jnp.zeros((nc, 1), jnp.int32)
    base = 0
    for n_g, l_g in zip(ncs, lseqs):
        mine = (row >= base) & (row < base + n_g)
        pos = jnp.where(mine, (row - base) % l_g, pos)
        lseq = jnp.where(mine, l_g, lseq)
        base += n_g

    def cmul_rows(tr, ti, k, s):
        ar, ai = _cexp(tr, ti, k)
        a_dup = jnp.concatenate([ar, ar], axis=1)
        a_sgn = jnp.concatenate([-ai, ai], axis=1)
        return a_dup * s + a_sgn * pltpu.roll(s, P, 1)

    carries = []
    for d in range(2):
        tr, ti = theta_rows[d]
        hs = summ[:, 2 * P * d:2 * P * (d + 1)]
        span = 1
        while span < max(lseqs):
            if d == 0:
                sh = jnp.where(pos >= span, pltpu.roll(hs, span, 0), 0.0)
            else:
                sh = jnp.where(pos + span < lseq, pltpu.roll(hs, nc - span, 0), 0.0)
            hs = hs + cmul_rows(tr, ti, float(CHUNK * span), sh)
            span *= 2
        if d == 0:
            carries.append(jnp.where(pos >= 1, pltpu.roll(hs, 1, 0), 0.0))
        else:
            carries.append(jnp.where(pos + 1 < lseq, pltpu.roll(hs, nc - 1, 0), 0.0))

    y = jnp.dot(x, m_scr[...], preferred_element_type=F32)
    y = y + jnp.dot(jnp.concatenate(carries, axis=1).astype(BF16), e_scr[...], preferred_element_type=F32)
    y = y + dvec_ref[0] * x.astype(F32)
    y = jax.nn.gelu(y)
    base = 0
    for o_ref, n_g in zip(o_refs, ncs):
        for ho in range(H):
            o_ref[0, ho] = y[base:base + n_g, ho * CHUNK:(ho + 1) * CHUNK].astype(BF16)
        base += n_g


def _ssm(u4s, prow, pcol, bt, c, ct, dvec, seqs):
    g = u4s[0].shape[0]
    ncs = tuple(u.shape[2] for u in u4s)
    lseqs = tuple(sq // CHUNK for sq in seqs)
    per_g = lambda *blk: pl.BlockSpec((1,) + blk, lambda i: (i,) + (0,) * len(blk))
    return pl.pallas_call(
        functools.partial(_ssm_kernel, ncs=ncs, lseqs=lseqs),
        grid=(g,),
        in_specs=[per_g(H, n_g, CHUNK) for n_g in ncs] + [
            per_g(8, P), per_g(P, 8), per_g(4 * H, P), per_g(4 * H, P), per_g(P, 4 * H), per_g(1, D_FLAT)],
        out_specs=[per_g(H, n_g, CHUNK) for n_g in ncs],
        out_shape=[jax.ShapeDtypeStruct(u.shape, BF16) for u in u4s],
        scratch_shapes=[pltpu.VMEM((H * H, 2 * CHUNK), F32),
                        pltpu.VMEM((D_FLAT, D_FLAT), BF16),
                        pltpu.VMEM((D_FLAT, 4 * P), BF16),
                        pltpu.VMEM((4 * P, D_FLAT), BF16)],
        compiler_params=_cparams(("arbitrary",)),
        name="ssm",
    )(*u4s, prow, pcol, bt, c, ct, dvec)


MIX_TB = 256


def _mix_kernel(attn_ref, g_ref, ga_ref, gs_ref, wglu_ref, wa_ref, ws_ref, o_ref):
    glu = jnp.dot(g_ref[...], wglu_ref[...], preferred_element_type=F32)
    ssm_out = (glu[:, :D_SSM] * jax.nn.sigmoid(glu[:, D_SSM:])).astype(BF16)
    a = jnp.dot(attn_ref[...], wa_ref[...], preferred_element_type=F32)
    s = jnp.dot(ssm_out, ws_ref[...], preferred_element_type=F32)
    merged = jax.nn.sigmoid(ga_ref[...].astype(F32)) * a + jax.nn.sigmoid(gs_ref[...].astype(F32)) * s
    o_ref[...] = merged.astype(BF16)


def _mix(attn, g, proj, w_glu, w_a, w_s):
    n = attn.shape[0]
    full = lambda r, c_: pl.BlockSpec((r, c_), lambda i: (0, 0))
    return pl.pallas_call(
        _mix_kernel,
        grid=(n // MIX_TB,),
        in_specs=[
            pl.BlockSpec((MIX_TB, D_ATTN), lambda i: (i, 0)),
            pl.BlockSpec((MIX_TB, D_SSM), lambda i: (i, 0)),
            pl.BlockSpec((MIX_TB, D_MODEL), lambda i: (i, GATE_A_COL)),
            pl.BlockSpec((MIX_TB, D_MODEL), lambda i: (i, GATE_A_COL + 1)),
            full(D_SSM, 2 * D_SSM), full(D_ATTN, D_MODEL), full(D_SSM, D_MODEL),
        ],
        out_specs=pl.BlockSpec((MIX_TB, D_MODEL), lambda i: (i, 0)),
        out_shape=jax.ShapeDtypeStruct((n, D_MODEL), BF16),
        compiler_params=_cparams(("arbitrary",)),
        name="mix",
    )(attn, g, proj, proj, w_glu, w_a, w_s)


OUT_TB = 512


def _out_kernel(m_ref, x_ref, wout_ref, gffn_ref, wrt_ref, x2_ref, xn_ref, afft_ref):
    x2 = x_ref[...] + jnp.dot(m_ref[...], wout_ref[...], preferred_element_type=F32)
    x2_ref[...] = x2
    xn = x2 * lax.rsqrt(jnp.mean(x2 * x2, axis=-1, keepdims=True) + EPS) * gffn_ref[...]
    xn_ref[...] = xn
    logits = lax.dot_general(wrt_ref[...], xn.astype(BF16), (((1,), (1,)), ((), ())),
                             preferred_element_type=F32)
    ex = jnp.exp(logits - jnp.max(logits, axis=0, keepdims=True))
    afft_ref[...] = ex / jnp.sum(ex, axis=0, keepdims=True)


def _out_proj(merged, x2d, w_out, g_ffn, w_rt):
    n = x2d.shape[0]
    return pl.pallas_call(
        _out_kernel,
        grid=(n // OUT_TB,),
        in_specs=[
            pl.BlockSpec((OUT_TB, D_MODEL), lambda i: (i, 0)),
            pl.BlockSpec((OUT_TB, D_MODEL), lambda i: (i, 0)),
            pl.BlockSpec((D_MODEL, D_MODEL), lambda i: (0, 0), pipeline_mode=pl.Buffered(1)),
            pl.BlockSpec((1, D_MODEL), lambda i: (0, 0)),
            pl.BlockSpec((N_EXPERTS, D_MODEL), lambda i: (0, 0)),
        ],
        out_specs=[
            pl.BlockSpec((OUT_TB, D_MODEL), lambda i: (i, 0)),
            pl.BlockSpec((OUT_TB, D_MODEL), lambda i: (i, 0)),
            pl.BlockSpec((N_EXPERTS, OUT_TB), lambda i: (0, i)),
        ],
        out_shape=[jax.ShapeDtypeStruct((n, D_MODEL), F32),
                   jax.ShapeDtypeStruct((n, D_MODEL), F32),
                   jax.ShapeDtypeStruct((N_EXPERTS, n), F32)],
        compiler_params=_cparams(("arbitrary",)),
        name="out_proj",
    )(merged, x2d, w_out, g_ffn, w_rt)


F32_INF_BITS = 0x7F800000
VALUE_BISECT_STEPS = 31


def _route_thr_kernel(aff_ref, thr_ref, cut_ref, *, cap, ne):
    nb = aff_ref.shape[0] // ne
    bits = pltpu.bitcast(aff_ref[...], jnp.int32)
    tok = (lax.broadcasted_iota(jnp.int32, (nb, LANES), 0) * LANES
           + lax.broadcasted_iota(jnp.int32, (nb, LANES), 1))
    seg_r = lax.broadcasted_iota(jnp.int32, (ne, ne * nb), 0)
    seg_c = lax.broadcasted_iota(jnp.int32, (ne, ne * nb), 1)
    seg = jnp.where((seg_c >= seg_r * nb) & (seg_c < (seg_r + 1) * nb), 1.0, 0.0).astype(BF16)
    ones = jnp.ones((LANES, LANES), BF16)

    def count(masks):
        m = jnp.concatenate([jnp.where(mk, 1.0, 0.0) for mk in masks], axis=0).astype(BF16)
        rows = jnp.dot(m, ones, preferred_element_type=F32)
        return jnp.dot(seg, rows.astype(BF16), preferred_element_type=F32)

    def per_expert(fn):
        return [fn(bits[e * nb:(e + 1) * nb], e) for e in range(ne)]

    def vstep(_, lohi):
        lo, hi = lohi
        mid = lo + ((hi - lo + 1) >> 1)
        ok = count(per_expert(lambda b, e: b >= mid[e:e + 1])) >= cap
        return jnp.where(ok, mid, lo), jnp.where(ok, hi, mid - 1)

    thr, _ = lax.fori_loop(0, VALUE_BISECT_STEPS, vstep,
                           (jnp.zeros((ne, LANES), jnp.int32), jnp.full((ne, LANES), F32_INF_BITS, jnp.int32)))
    need = cap - count(per_expert(lambda b, e: b > thr[e:e + 1]))

    def tstep(_, lohi):
        lo, hi = lohi
        mid = lo + ((hi - lo) >> 1)
        ok = count(per_expert(lambda b, e: (b == thr[e:e + 1]) & (tok <= mid[e:e + 1]))) >= need
        return jnp.where(ok, lo, mid + 1), jnp.where(ok, mid, hi)

    n_tok_bits = max(1, (nb * LANES - 1).bit_length())
    cut, _ = lax.fori_loop(0, n_tok_bits, tstep,
                           (jnp.zeros((ne, LANES), jnp.int32), jnp.full((ne, LANES), nb * LANES - 1, jnp.int32)))
    thr_ref[...] = thr
    cut_ref[...] = cut


def _route_compact_kernel(aff_ref, thr_ref, cut_ref, idx_ref, gate_ref, *, cap):
    aff = aff_ref[0]
    nb = aff.shape[0]
    bits = pltpu.bitcast(aff, jnp.int32)
    tok = (lax.broadcasted_iota(jnp.int32, (nb, LANES), 0) * LANES
           + lax.broadcasted_iota(jnp.int32, (nb, LANES), 1))
    thr, cut = thr_ref[0], cut_ref[0]
    sel = jnp.where((bits > thr) | ((bits == thr) & (tok <= cut)), 1.0, 0.0).astype(BF16)

    r_i = lax.broadcasted_iota(jnp.int32, (LANES, LANES), 0)
    c_i = lax.broadcasted_iota(jnp.int32, (LANES, LANES), 1)
    upper = jnp.where(r_i <= c_i, 1.0, 0.0).astype(BF16)
    cw = jnp.dot(sel, upper, preferred_element_type=F32)
    cnt_row = lax.dot_general(jnp.ones((8, LANES), BF16), sel, (((1,), (1,)), ((), ())),
                              preferred_element_type=F32)
    rb = lax.broadcasted_iota(jnp.int32, (nb, nb), 0)
    cb = lax.broadcasted_iota(jnp.int32, (nb, nb), 1)
    upper_nb = jnp.where(rb <= cb, 1.0, 0.0).astype(BF16)
    incl_row = jnp.dot(cnt_row.astype(BF16), upper_nb, preferred_element_type=F32)[0:1]
    excl_row = incl_row - cnt_row[0:1]

    slot = lax.broadcasted_iota(jnp.int32, (cap, 1), 0).astype(F32)
    blk = jnp.sum(jnp.where(incl_row <= slot, 1.0, 0.0), axis=1, keepdims=True)
    onehot_b = lax.broadcasted_iota(jnp.int32, (cap, nb), 1).astype(F32) == blk
    base = jnp.sum(jnp.where(onehot_b, excl_row, 0.0), axis=1, keepdims=True)
    oh = jnp.where(onehot_b, 1.0, 0.0).astype(BF16)
    cw_j = jnp.dot(oh, cw.astype(BF16), preferred_element_type=F32)
    loc = jnp.sum(jnp.where(cw_j <= slot - base, 1.0, 0.0), axis=1, keepdims=True)
    idx = (blk * LANES + loc).astype(jnp.int32)
    idx_ref[0] = jnp.broadcast_to(idx, (cap, LANES))

    a_hi = aff.astype(BF16)
    r1 = aff - a_hi.astype(F32)
    a_mid = r1.astype(BF16)
    a_lo = (r1 - a_mid.astype(F32)).astype(BF16)
    aff_j = (jnp.dot(oh, a_hi, preferred_element_type=F32) + jnp.dot(oh, a_mid, preferred_element_type=F32)
             + jnp.dot(oh, a_lo, preferred_element_type=F32))
    lane = lax.broadcasted_iota(jnp.int32, (cap, LANES), 1).astype(F32)
    gate = jnp.sum(jnp.where(lane == loc, aff_j, 0.0), axis=1, keepdims=True)
    gate_ref[0] = jnp.broadcast_to(gate, (cap, LANES))


def _route(aff3, cap):
    e, nb, _ = aff3.shape
    thr, cut = pl.pallas_call(
        functools.partial(_route_thr_kernel, cap=cap, ne=e),
        grid=(1,),
        in_specs=[pl.BlockSpec((e * nb, LANES), lambda i: (0, 0))],
        out_specs=[pl.BlockSpec((e, LANES), lambda i: (0, 0))] * 2,
        out_shape=[jax.ShapeDtypeStruct((e, LANES), jnp.int32)] * 2,
        compiler_params=_cparams(("arbitrary",)),
        name="route_thr",
    )(aff3.reshape(e * nb, LANES))
    row = pl.BlockSpec((1, 1, LANES), lambda i: (i, 0, 0))
    return pl.pallas_call(
        functools.partial(_route_compact_kernel, cap=cap),
        grid=(e,),
        in_specs=[pl.BlockSpec((1, nb, LANES), lambda i: (i, 0, 0)), row, row],
        out_specs=[pl.BlockSpec((1, cap, LANES), lambda i: (i, 0, 0)),
                   pl.BlockSpec((1, cap, LANES), lambda i: (i, 0, 0))],
        out_shape=[jax.ShapeDtypeStruct((e, cap, LANES), jnp.int32),
                   jax.ShapeDtypeStruct((e, cap, LANES), F32)],
        compiler_params=_cparams(("arbitrary",)),
        name="route_compact",
    )(aff3, thr.reshape(e, 1, LANES), cut.reshape(e, 1, LANES))


FFN_TM = 256


def _ffn_kernel(idxp_ref, idxc_ref, idxn_ref, gate_ref, xn_hbm, acc_in_hbm, wg_hbm, wu_hbm, wd_hbm, acc_hbm,
                w_scr, xbuf, abuf, obuf, xbf, sem_w, sem_x, sem_a, sem_o, *, tm, nt, ne):
    del acc_in_hbm
    e, r = pl.program_id(0), pl.program_id(1)
    s = e * nt + r
    slot = s % 2
    other = 1 - slot
    last_step = ne * nt - 1

    def x_row(idx_ref, j, sl):
        return pltpu.make_async_copy(xn_hbm.at[pl.ds(idx_ref[0, 0, 0, j], 1), :],
                                     xbuf.at[sl, pl.ds(j, 1), :], sem_x.at[sl])

    def a_row(idx_ref, j, sl):
        return pltpu.make_async_copy(acc_hbm.at[pl.ds(idx_ref[0, 0, 0, j], 1), :],
                                     abuf.at[sl, pl.ds(j, 1), :], sem_a.at[sl])

    def o_row(idx_ref, j, sl):
        return pltpu.make_async_copy(obuf.at[sl, pl.ds(j, 1), :],
                                     acc_hbm.at[pl.ds(idx_ref[0, 0, 0, j], 1), :], sem_o.at[sl])

    def wait_tile(buf, sem, sl):
        pltpu.make_async_copy(buf.at[sl], buf.at[sl], sem.at[sl]).wait()

    @pl.when(r == 0)
    def _():
        copies = [pltpu.make_async_copy(w.at[e], w_scr.at[k], sem_w.at[k])
                  for k, w in enumerate((wg_hbm, wu_hbm, wd_hbm))]
        for cp in copies:
            cp.start()
        for cp in copies:
            cp.wait()

    @pl.when(s == 0)
    def _():
        def first(j, c_):
            x_row(idxc_ref, j, 0).start()
            a_row(idxc_ref, j, 0).start()
            return c_
        lax.fori_loop(0, tm, first, 0)

    wait_tile(xbuf, sem_x, slot)

    @pl.when((r >= 2) | ((r == 0) & (s >= 2)))
    def _():
        wait_tile(obuf, sem_o, slot)

    xbf[...] = xbuf[slot].astype(BF16)

    def compute(with_acc):
        x = xbf[...]
        hid = (jax.nn.silu(jnp.dot(x, w_scr[0], preferred_element_type=F32))
               * jnp.dot(x, w_scr[1], preferred_element_type=F32))
        y = jnp.dot(hid.astype(BF16), w_scr[2], preferred_element_type=F32) * gate_ref[0][:, 0:1]
        obuf[slot] = (abuf[slot] + y) if with_acc else y

    def start_gather_next(with_acc):
        for j in range(tm):
            x_row(idxn_ref, j, other).start()
            if with_acc:
                a_row(idxn_ref, j, other).start()

    @pl.when(r == 0)
    def _():
        @pl.when(s > 0)
        def _():
            def writeback_prev(j, c_):
                o_row(idxp_ref, j, other).start()
                return c_
            lax.fori_loop(0, tm, writeback_prev, 0)
        start_gather_next(True)
        compute(False)

        @pl.when(s > 0)
        def _():
            wait_tile(obuf, sem_o, other)
            def gather(j, c_):
                a_row(idxc_ref, j, slot).start()
                return c_
            lax.fori_loop(0, tm, gather, 0)
        wait_tile(abuf, sem_a, slot)
        obuf[slot] = obuf[slot] + abuf[slot]

    @pl.when(r > 0)
    def _():
        wait_tile(abuf, sem_a, slot)

    @pl.when((r > 0) & (r < nt - 1))
    def _():
        for j in range(tm):
            o_row(idxp_ref, j, other).start()
        start_gather_next(True)
        compute(True)

    @pl.when(r == nt - 1)
    def _():
        for j in range(tm):
            o_row(idxp_ref, j, other).start()
        start_gather_next(False)
        compute(True)

    @pl.when(s == last_step)
    def _():
        def writeback(j, c_):
            o_row(idxc_ref, j, slot).start()
            return c_
        lax.fori_loop(0, tm, writeback, 0)
        wait_tile(obuf, sem_o, slot)
        wait_tile(obuf, sem_o, other)
        wait_tile(xbuf, sem_x, other)


def _ffn(idx4, gate, xn, acc, w_gate, w_up, w_down):
    e, nt, _, tm = idx4.shape
    assert nt >= 2
    anyspec = pl.BlockSpec(memory_space=pl.ANY)

    def prev(i, r):
        s = jnp.maximum(i * nt + r - 1, 0)
        return (s // nt, s % nt, 0, 0)

    def nxt(i, r):
        s = jnp.minimum(i * nt + r + 1, e * nt - 1)
        return (s // nt, s % nt, 0, 0)

    idx_spec = lambda f: pl.BlockSpec((1, 1, 1, tm), f, memory_space=pltpu.SMEM)
    return pl.pallas_call(
        functools.partial(_ffn_kernel, tm=tm, nt=nt, ne=e),
        grid=(e, nt),
        in_specs=[
            idx_spec(prev), idx_spec(lambda i, r: (i, r, 0, 0)), idx_spec(nxt),
            pl.BlockSpec((1, tm, LANES), lambda i, r: (i, r, 0)),
            anyspec, anyspec, anyspec, anyspec, anyspec,
        ],
        out_specs=anyspec,
        out_shape=jax.ShapeDtypeStruct(acc.shape, F32),
        scratch_shapes=[pltpu.VMEM((3, D_MODEL, D_MODEL), BF16),
                        pltpu.VMEM((2, tm, D_MODEL), F32),
                        pltpu.VMEM((2, tm, D_MODEL), F32),
                        pltpu.VMEM((2, tm, D_MODEL), F32),
                        pltpu.VMEM((tm, D_MODEL), BF16),
                        pltpu.SemaphoreType.DMA((3,)),
                        pltpu.SemaphoreType.DMA((2,)),
                        pltpu.SemaphoreType.DMA((2,)),
                        pltpu.SemaphoreType.DMA((2,))],
        input_output_aliases={5: 0},
        compiler_params=_cparams(("arbitrary", "arbitrary")),
        name="ffn",
    )(idx4, idx4, idx4, gate, xn, acc, w_gate, w_up, w_down)


def _prep_weights(g_mix, w_in, g_q, g_k, attn_sink, lam_re, lam_im, log_step, b_re, b_im, c_re, c_im,
                  d_skip, w_glu, w_br_attn, w_br_ssm, w_out, g_ffn, w_router, w_gate, w_up, w_down):
    u0 = D_ATTN + 2 * D_KV
    w_main = jnp.concatenate([w_in[:, :D_ATTN], w_in[:, u0 + D_SSM:], w_in[:, D_ATTN:u0]], axis=1).astype(BF16)
    w_ut = w_in[:, u0:u0 + D_SSM].T.astype(BF16)
    g_ = N_SSM_GROUPS
    ls = jnp.broadcast_to(log_step[:, :, None], (2, g_, P))
    zeros = jnp.zeros((g_, P), F32)
    prow = jnp.stack([lam_re[0], lam_im[0], ls[0], lam_re[1], lam_im[1], ls[1], zeros, zeros], axis=1)
    pcol = jnp.swapaxes(prow, 1, 2)
    tr = lambda a: jnp.swapaxes(a, 1, 2)
    bt = jnp.concatenate([tr(b_re[0]), tr(b_im[0]), tr(b_re[1]), tr(b_im[1])], axis=1)
    c = jnp.concatenate([c_re[0], c_im[0], c_re[1], c_im[1]], axis=1)
    ct = jnp.swapaxes(c, 1, 2)
    dvec = jnp.repeat(d_skip.reshape(g_, H), CHUNK, axis=1).reshape(g_, 1, D_FLAT)
    return dict(
        g_mix=g_mix.reshape(1, D_MODEL), w_main=w_main, w_ut=w_ut,
        g_q=g_q.reshape(1, HEAD_DIM), g_k=g_k.reshape(1, HEAD_DIM), sink=attn_sink,
        prow=prow, pcol=pcol, bt=bt, c=c, ct=ct, dvec=dvec,
        w_glu=w_glu.astype(BF16), w_a=w_br_attn.astype(BF16), w_s=w_br_ssm.astype(BF16),
        w_out=w_out.astype(BF16), g_ffn=g_ffn.reshape(1, D_MODEL), w_rt=w_router.T.astype(BF16),
        w_gate=w_gate.astype(BF16), w_up=w_up.astype(BF16), w_down=w_down.astype(BF16))


def _layers(xs, w):
    shapes = [x.shape for x in xs]
    x2ds = [x.reshape(b * s, D_MODEL) for x, (b, s, _) in zip(xs, shapes)]
    pre = [_in_proj(x2d, w["g_mix"], w["w_main"], w["w_ut"]) for x2d in x2ds]
    u4s = [ut.reshape(N_SSM_GROUPS, H, ut.shape[1] // CHUNK, CHUNK) for _, ut in pre]
    g4s = _ssm(u4s, w["prow"], w["pcol"], w["bt"], w["c"], w["ct"], w["dvec"], [s for _, s, _ in shapes])
    outs = []
    for (b, s, _), x2d, (proj, _), g4 in zip(shapes, x2ds, pre, g4s):
        n = b * s
        attn = _attention(proj, w["g_q"], w["g_k"], w["sink"], s)
        g = g4.reshape(D_SSM, n).T
        merged = _mix(attn, g, proj, w["w_glu"], w["w_a"], w["w_s"])
        x2, xn, afft = _out_proj(merged, x2d, w["w_out"], w["g_ffn"], w["w_rt"])
        cap = max(1, min(n, EC_CAPACITY_FACTOR * n // N_EXPERTS))
        idx, gate = _route(afft.reshape(N_EXPERTS, n // LANES, LANES), cap)
        tm = min(FFN_TM, cap // 2)
        idx4 = idx[:, :, 0].reshape(N_EXPERTS, cap // tm, 1, tm)
        y = _ffn(idx4, gate, xn, x2, w["w_gate"], w["w_up"], w["w_down"])
        outs.append(y.reshape(b, s, D_MODEL))
    return outs


def kernel(x_prompt, x_sample, g_mix, w_in, g_q, g_k, attn_sink, lam_re, lam_im, log_step, b_re, b_im,
           c_re, c_im, d_skip, w_glu, w_br_attn, w_br_ssm, w_out, g_ffn, w_router, w_gate, w_up, w_down):
    depth = g_mix.shape[0]
    y_prompt, y_sample = x_prompt, x_sample
    for l in range(depth):
        w = _prep_weights(g_mix[l], w_in[l], g_q[l], g_k[l], attn_sink[l], lam_re[l], lam_im[l],
                          log_step[l], b_re[l], b_im[l], c_re[l], c_im[l], d_skip[l], w_glu[l],
                          w_br_attn[l], w_br_ssm[l], w_out[l], g_ffn[l], w_router[l], w_gate[l],
                          w_up[l], w_down[l])
        y_prompt, y_sample = _layers([y_prompt, y_sample], w)
    return (y_prompt, y_sample)
```

```python
import functools
import math

import jax
import jax.numpy as jnp
from jax import lax
from jax.experimental import pallas as pl
from jax.experimental.pallas import tpu as pltpu

F32 = jnp.float32
BF16 = jnp.bfloat16

D_MODEL = 2048
N_Q_HEADS = 16
N_KV_HEADS = 4
HEAD_DIM = 128
Q_GROUP = N_Q_HEADS // N_KV_HEADS
D_ATTN = N_Q_HEADS * HEAD_DIM
D_KV = N_KV_HEADS * HEAD_DIM
WINDOW = 128
BLOCK = 128
D_SSM = 1024
SSM_GROUP = 16
N_SSM_GROUPS = 64
SSM_STATE = 64
N_EXPERTS = 16
EC_CAPACITY_FACTOR = 2
EPS = 1e-6
NEG_INF = -1e30

CHUNK = 128
LANES = 128
D_PROJ = D_ATTN + 2 * D_KV + 2 * D_MODEL
GATE_A_COL = D_ATTN // D_MODEL
K_COL = D_ATTN + 2 * D_MODEL
LOG2E = math.log2(math.e)
VMEM_LIMIT_BYTES = 56 * 1024 * 1024


def _cparams(sem):
    return pltpu.CompilerParams(dimension_semantics=sem, vmem_limit_bytes=VMEM_LIMIT_BYTES)


IN_TB = 1024
IN_TN = 1024
IN_NJ = D_PROJ // IN_TN
Q_TILES = D_ATTN // IN_TN
assert 2 * D_KV == IN_TN


def _qk_norm(acc, gain, n_heads):
    parts = []
    for h in range(acc.shape[1] // HEAD_DIM):
        seg = acc[:, h * HEAD_DIM:(h + 1) * HEAD_DIM]
        if h < n_heads:
            seg = seg * lax.rsqrt(jnp.mean(seg * seg, axis=-1, keepdims=True) + EPS) * gain
        parts.append(seg)
    return jnp.concatenate(parts, axis=1)


def _inproj_kernel(x_ref, g_ref, gq_ref, gk_ref, w_ref, wut_ref, proj_ref, ut_ref, xn_scr):
    j = pl.program_id(1)

    @pl.when(j == 0)
    def _():
        x = x_ref[...]
        ms = jnp.mean(x * x, axis=-1, keepdims=True)
        xn_scr[...] = (x * lax.rsqrt(ms + EPS) * g_ref[...]).astype(BF16)

    def project():
        return jnp.dot(xn_scr[...], w_ref[...], preferred_element_type=F32)

    @pl.when(j < Q_TILES)
    def _():
        gain = gq_ref[...] * (HEAD_DIM ** -0.5 * LOG2E)
        proj_ref[...] = _qk_norm(project(), gain, IN_TN // HEAD_DIM).astype(BF16)

    @pl.when((j >= Q_TILES) & (j < IN_NJ - 1))
    def _():
        proj_ref[...] = project().astype(BF16)

    @pl.when(j == IN_NJ - 1)
    def _():
        proj_ref[...] = _qk_norm(project(), gk_ref[...], N_KV_HEADS).astype(BF16)

    @pl.when(j == IN_NJ)
    def _():
        ut_ref[...] = lax.dot_general(wut_ref[...], xn_scr[...], (((1,), (1,)), ((), ())),
                                      preferred_element_type=F32).astype(BF16)


def _in_proj(x2d, g_mix, g_q, g_k, w_main, w_ut):
    n = x2d.shape[0]
    last = IN_NJ - 1
    return pl.pallas_call(
        _inproj_kernel,
        grid=(n // IN_TB, IN_NJ + 1),
        in_specs=[
            pl.BlockSpec((IN_TB, D_MODEL), lambda i, j: (i, 0)),
            pl.BlockSpec((1, D_MODEL), lambda i, j: (0, 0)),
            pl.BlockSpec((1, HEAD_DIM), lambda i, j: (0, 0)),
            pl.BlockSpec((1, HEAD_DIM), lambda i, j: (0, 0)),
            pl.BlockSpec((D_MODEL, IN_TN), lambda i, j: (0, jnp.minimum(j, last))),
            pl.BlockSpec((D_SSM, D_MODEL), lambda i, j: (0, 0), pipeline_mode=pl.Buffered(1)),
        ],
        out_specs=[
            pl.BlockSpec((IN_TB, IN_TN), lambda i, j: (i, jnp.minimum(j, last))),
            pl.BlockSpec((D_SSM, IN_TB), lambda i, j: (0, i)),
        ],
        out_shape=[jax.ShapeDtypeStruct((n, D_PROJ), BF16),
                   jax.ShapeDtypeStruct((D_SSM, n), BF16)],
        scratch_shapes=[pltpu.VMEM((IN_TB, D_MODEL), BF16)],
        compiler_params=_cparams(("arbitrary", "arbitrary")),
        name="in_proj",
    )(x2d, g_mix, g_q, g_k, w_main, w_ut)


KEYS = 3 * BLOCK
ALIBI_SLOPES = tuple(2.0 ** (-8.0 * (h + 1) / N_Q_HEADS) for h in range(N_Q_HEADS))


def _attn_kernel(sink_ref, q_ref, kp_ref, kc_ref, kn_ref, vp_ref, vc_ref, vn_ref, o_ref, bias_scr, *, nb_seq):
    rows = Q_GROUP * BLOCK

    @pl.when(pl.program_id(0) == 0)
    def _():
        qi = lax.broadcasted_iota(jnp.int32, (BLOCK, KEYS), 0)
        kj = lax.broadcasted_iota(jnp.int32, (BLOCK, KEYS), 1)
        dist = jnp.abs(qi - kj + BLOCK)
        distf = dist.astype(F32)
        for h in range(N_Q_HEADS):
            bias_scr[h] = jnp.where(dist <= WINDOW, distf * (-ALIBI_SLOPES[h] * LOG2E), NEG_INF)

    pos = pl.program_id(0) % nb_seq
    k_lo = jnp.where(pos == 0, BLOCK, 0)
    k_hi = jnp.where(pos == nb_seq - 1, 2 * BLOCK, KEYS)
    kj1 = lax.broadcasted_iota(jnp.int32, (1, KEYS), 1)
    edge = jnp.where((kj1 >= k_lo) & (kj1 < k_hi), 0.0, NEG_INF)
    head_of_row = lax.broadcasted_iota(jnp.int32, (rows, 1), 0) // BLOCK
    ones = jnp.ones((KEYS, HEAD_DIM), BF16)

    for kh in range(N_KV_HEADS):
        cs = slice(kh * HEAD_DIM, (kh + 1) * HEAD_DIM)
        kn = jnp.concatenate([kp_ref[:, cs], kc_ref[:, cs], kn_ref[:, cs]], axis=0)
        v1 = jnp.concatenate([jnp.concatenate([vp_ref[:, cs], vc_ref[:, cs], vn_ref[:, cs]], axis=0), ones],
                             axis=1)
        sink = jnp.zeros((rows, 1), F32)
        for j in range(Q_GROUP):
            sink = jnp.where(head_of_row == j, sink_ref[kh * Q_GROUP + j] * LOG2E, sink)
        qg = jnp.concatenate([q_ref[:, (kh * Q_GROUP + j) * HEAD_DIM:(kh * Q_GROUP + j + 1) * HEAD_DIM]
                              for j in range(Q_GROUP)], axis=0)
        s = lax.dot_general(qg, kn, (((1,), (1,)), ((), ())), preferred_element_type=F32)
        bias = jnp.concatenate([bias_scr[kh * Q_GROUP + j] for j in range(Q_GROUP)], axis=0)
        s = s + bias + edge
        m = jnp.maximum(jnp.max(s, axis=-1, keepdims=True), sink)
        p = jnp.exp2(s - m).astype(BF16)
        pv = jnp.dot(p, v1, preferred_element_type=F32)
        den = pv[:, HEAD_DIM:HEAD_DIM + 1] + jnp.exp2(sink - m)
        o = pv[:, :HEAD_DIM] / den
        for j in range(Q_GROUP):
            h = kh * Q_GROUP + j
            o_ref[:, h * HEAD_DIM:(h + 1) * HEAD_DIM] = o[j * BLOCK:(j + 1) * BLOCK].astype(BF16)


def _attention(proj, sink, seq):
    n = proj.shape[0]
    nb_seq = seq // BLOCK
    kcol = K_COL // D_KV
    vcol = kcol + 1

    def prev(i):
        return jnp.where(i % nb_seq == 0, i, i - 1)

    def nxt(i):
        return jnp.where(i % nb_seq == nb_seq - 1, i, i + 1)

    kv = lambda col, f: pl.BlockSpec((BLOCK, D_KV), lambda i: (f(i), col))
    same = lambda i: i
    return pl.pallas_call(
        functools.partial(_attn_kernel, nb_seq=nb_seq),
        grid=(n // BLOCK,),
        in_specs=[
            pl.BlockSpec(memory_space=pltpu.SMEM),
            pl.BlockSpec((BLOCK, D_ATTN), lambda i: (i, 0)),
            kv(kcol, prev), kv(kcol, same), kv(kcol, nxt),
            kv(vcol, prev), kv(vcol, same), kv(vcol, nxt),
        ],
        out_specs=pl.BlockSpec((BLOCK, D_ATTN), lambda i: (i, 0)),
        out_shape=jax.ShapeDtypeStruct((n, D_ATTN), BF16),
        scratch_shapes=[pltpu.VMEM((N_Q_HEADS, BLOCK, KEYS), F32)],
        compiler_params=_cparams(("arbitrary",)),
        name="attention",
    )(sink, proj, proj, proj, proj, proj, proj, proj)


H = SSM_GROUP
P = SSM_STATE
D_FLAT = H * CHUNK


def _cexp(tr, ti, k):
    mag = jnp.exp(tr * k)
    return mag * jnp.cos(ti * k), mag * jnp.sin(ti * k)


def _ssm_kernel(*refs, ncs, lseqs):
    ng = len(ncs)
    x_refs = refs[:ng]
    prow_ref, pcol_ref, bt_ref, c_ref, ct_ref, dvec_ref = refs[ng:ng + 6]
    o_refs = refs[ng + 6:2 * ng + 6]
    kk_scr, m_scr, f_scr, e_scr = refs[2 * ng + 6:]
    nc = sum(ncs)
    prow = prow_ref[0]
    pcol = pcol_ref[0]
    bt = bt_ref[0]
    c = c_ref[0]
    ct = ct_ref[0]
    lag = lax.broadcasted_iota(jnp.int32, (1, CHUNK), 1).astype(F32)
    tpos = lax.broadcasted_iota(jnp.int32, (CHUNK, 1), 0).astype(F32)

    theta_rows = []
    for d in range(2):
        lr, li = prow[3 * d:3 * d + 1], prow[3 * d + 1:3 * d + 2]
        st = jnp.exp(prow[3 * d + 2:3 * d + 3])
        tr, ti = lr * st, li * st
        theta_rows.append((tr, ti))
        ar, ai = _cexp(tr, ti, 1.0)
        nr, ni = ar - 1.0, ai
        den = lr * lr + li * li
        cr, ci = (nr * lr + ni * li) / den, (ni * lr - nr * li) / den
        b_re, b_im = bt[2 * H * d:2 * H * d + H], bt[2 * H * d + H:2 * H * (d + 1)]
        bb_re, bb_im = cr * b_re - ci * b_im, cr * b_im + ci * b_re

        lrc, lic = pcol[:, 3 * d:3 * d + 1], pcol[:, 3 * d + 1:3 * d + 2]
        stc = jnp.exp(pcol[:, 3 * d + 2:3 * d + 3])
        trc, tic = lrc * stc, lic * stc

        cb_re, cb_im = [], []
        for ho in range(H):
            c_re, c_im = c[2 * H * d + ho:2 * H * d + ho + 1], c[2 * H * d + H + ho:2 * H * d + H + ho + 1]
            cb_re.append(c_re * bb_re - c_im * bb_im)
            cb_im.append(c_re * bb_im + c_im * bb_re)
        cb_re, cb_im = jnp.concatenate(cb_re, axis=0), jnp.concatenate(cb_im, axis=0)

        expo = lag if d == 0 else (CHUNK - lag)
        vr, vi = _cexp(trc, tic, expo)
        kt = jnp.dot(jnp.concatenate([cb_re, cb_im], axis=1), jnp.concatenate([vr, -vi], axis=0),
                     preferred_element_type=F32, precision=lax.Precision.HIGHEST)
        if d == 0:
            kk_scr[:, CHUNK:] = kt
        else:
            kk_scr[:, :CHUNK] = kt
            k0 = jnp.sum(cb_re, axis=1, keepdims=True)
            kk_scr[:, CHUNK:] = kk_scr[:, CHUNK:] + jnp.where(lag == 0.0, k0, 0.0)

        vsr, vsi = _cexp(tr, ti, (CHUNK - 1.0 - tpos) if d == 0 else tpos)
        for hi in range(H):
            fr = vsr * bb_re[hi:hi + 1] - vsi * bb_im[hi:hi + 1]
            fi = vsr * bb_im[hi:hi + 1] + vsi * bb_re[hi:hi + 1]
            f_scr[hi * CHUNK:(hi + 1) * CHUNK, 2 * P * d:2 * P * (d + 1)] = (
                jnp.concatenate([fr, fi], axis=1).astype(BF16))

        er, ei = _cexp(trc, tic, lag + 1.0) if d == 0 else (vr, vi)
        for ho in range(H):
            cc_re = ct[:, 2 * H * d + ho:2 * H * d + ho + 1]
            cc_im = ct[:, 2 * H * d + H + ho:2 * H * d + H + ho + 1]
            wr, wi = cc_re * er - cc_im * ei, cc_re * ei + cc_im * er
            e_scr[2 * P * d:2 * P * (d + 1), ho * CHUNK:(ho + 1) * CHUNK] = (
                jnp.concatenate([wr, -wi], axis=0).astype(BF16))

    def build(hi, carry):
        for ho in range(H):
            row = kk_scr[pl.ds(ho * H + hi, 1), :]
            rolled = pltpu.roll(jnp.broadcast_to(row, (CHUNK, 2 * CHUNK)), 0, 1, stride=1, stride_axis=0)
            m_scr[pl.ds(pl.multiple_of(hi * CHUNK, CHUNK), CHUNK), ho * CHUNK:(ho + 1) * CHUNK] = (
                rolled[:, CHUNK:].astype(BF16))
        return carry

    lax.fori_loop(0, H, build, 0)

    x = jnp.concatenate([jnp.concatenate([xr[0, hi] for hi in range(H)], axis=1) for xr in x_refs],
                        axis=0)
    summ = jnp.dot(x, f_scr[...], preferred_element_type=F32)
    row = lax.broadcasted_iota(jnp.int32, (nc, 1), 0)
    pos = jnp.zeros((nc, 1), jnp.int32)
    lseq = jnp.zeros((nc, 1), jnp.int32)
    base = 0
    for n_g, l_g in zip(ncs, lseqs):
        mine = (row >= base) & (row < base + n_g)
        pos = jnp.where(mine, (row - base) % l_g, pos)
        lseq = jnp.where(mine, l_g, lseq)
        base += n_g

    def cmul_rows(tr, ti, k, s):
        ar, ai = _cexp(tr, ti, k)
        a_dup = jnp.concatenate([ar, ar], axis=1)
        a_sgn = jnp.concatenate([-ai, ai], axis=1)
        return a_dup * s + a_sgn * pltpu.roll(s, P, 1)

    carries = []
    for d in range(2):
        tr, ti = theta_rows[d]
        hs = summ[:, 2 * P * d:2 * P * (d + 1)]
        span = 1
        while span < max(lseqs):
            if d == 0:
                sh = jnp.where(pos >= span, pltpu.roll(hs, span, 0), 0.0)
            else:
                sh = jnp.where(pos + span < lseq, pltpu.roll(hs, nc - span, 0), 0.0)
            hs = hs + cmul_rows(tr, ti, float(CHUNK * span), sh)
            span *= 2
        if d == 0:
            carries.append(jnp.where(pos >= 1, pltpu.roll(hs, 1, 0), 0.0))
        else:
            carries.append(jnp.where(pos + 1 < lseq, pltpu.roll(hs, nc - 1, 0), 0.0))

    y = jnp.dot(x, m_scr[...], preferred_element_type=F32)
    y = y + jnp.dot(jnp.concatenate(carries, axis=1).astype(BF16), e_scr[...], preferred_element_type=F32)
    y = y + dvec_ref[0] * x.astype(F32)
    y = jax.nn.gelu(y)
    base = 0
    for o_ref, n_g in zip(o_refs, ncs):
        for ho in range(H):
            o_ref[0, ho] = y[base:base + n_g, ho * CHUNK:(ho + 1) * CHUNK].astype(BF16)
        base += n_g


def _ssm(u4s, prow, pcol, bt, c, ct, dvec, seqs):
    g = u4s[0].shape[0]
    ncs = tuple(u.shape[2] for u in u4s)
    lseqs = tuple(sq // CHUNK for sq in seqs)
    per_g = lambda *blk: pl.BlockSpec((1,) + blk, lambda i: (i,) + (0,) * len(blk))
    return pl.pallas_call(
        functools.partial(_ssm_kernel, ncs=ncs, lseqs=lseqs),
        grid=(g,),
        in_specs=[per_g(H, n_g, CHUNK) for n_g in ncs] + [
            per_g(8, P), per_g(P, 8), per_g(4 * H, P), per_g(4 * H, P), per_g(P, 4 * H), per_g(1, D_FLAT)],
        out_specs=[per_g(H, n_g, CHUNK) for n_g in ncs],
        out_shape=[jax.ShapeDtypeStruct(u.shape, BF16) for u in u4s],
        scratch_shapes=[pltpu.VMEM((H * H, 2 * CHUNK), F32),
                        pltpu.VMEM((D_FLAT, D_FLAT), BF16),
                        pltpu.VMEM((D_FLAT, 4 * P), BF16),
                        pltpu.VMEM((4 * P, D_FLAT), BF16)],
        compiler_params=_cparams(("arbitrary",)),
        name="ssm",
    )(*u4s, prow, pcol, bt, c, ct, dvec)


MIX_TB = 256


def _mix_kernel(attn_ref, g_ref, ga_ref, gs_ref, wglu_ref, wa_ref, ws_ref, o_ref):
    glu = lax.dot_general(g_ref[...], wglu_ref[...], (((0,), (0,)), ((), ())),
                          preferred_element_type=F32)
    ssm_out = (glu[:, :D_SSM] * jax.nn.sigmoid(glu[:, D_SSM:])).astype(BF16)
    a = jnp.dot(attn_ref[...], wa_ref[...], preferred_element_type=F32)
    s = jnp.dot(ssm_out, ws_ref[...], preferred_element_type=F32)
    merged = jax.nn.sigmoid(ga_ref[...].astype(F32)) * a + jax.nn.sigmoid(gs_ref[...].astype(F32)) * s
    o_ref[...] = merged.astype(BF16)


def _mix(attn, g, proj, w_glu, w_a, w_s):
    n = attn.shape[0]
    full = lambda r, c_: pl.BlockSpec((r, c_), lambda i: (0, 0))
    return pl.pallas_call(
        _mix_kernel,
        grid=(n // MIX_TB,),
        in_specs=[
            pl.BlockSpec((MIX_TB, D_ATTN), lambda i: (i, 0)),
            pl.BlockSpec((D_SSM, MIX_TB), lambda i: (0, i)),
            pl.BlockSpec((MIX_TB, D_MODEL), lambda i: (i, GATE_A_COL)),
            pl.BlockSpec((MIX_TB, D_MODEL), lambda i: (i, GATE_A_COL + 1)),
            full(D_SSM, 2 * D_SSM), full(D_ATTN, D_MODEL), full(D_SSM, D_MODEL),
        ],
        out_specs=pl.BlockSpec((MIX_TB, D_MODEL), lambda i: (i, 0)),
        out_shape=jax.ShapeDtypeStruct((n, D_MODEL), BF16),
        compiler_params=_cparams(("arbitrary",)),
        name="mix",
    )(attn, g, proj, proj, w_glu, w_a, w_s)


OUT_TB = 512


def _out_kernel(m_ref, x_ref, wout_ref, gffn_ref, wrt_ref, x2_ref, xn_ref, afft_ref):
    x2 = x_ref[...] + jnp.dot(m_ref[...], wout_ref[...], preferred_element_type=F32)
    x2_ref[...] = x2
    xn = x2 * lax.rsqrt(jnp.mean(x2 * x2, axis=-1, keepdims=True) + EPS) * gffn_ref[...]
    xn_ref[...] = xn
    logits = lax.dot_general(wrt_ref[...], xn.astype(BF16), (((1,), (1,)), ((), ())),
                             preferred_element_type=F32)
    ex = jnp.exp(logits - jnp.max(logits, axis=0, keepdims=True))
    afft_ref[...] = ex / jnp.sum(ex, axis=0, keepdims=True)


def _out_proj(merged, x2d, w_out, g_ffn, w_rt):
    n = x2d.shape[0]
    return pl.pallas_call(
        _out_kernel,
        grid=(n // OUT_TB,),
        in_specs=[
            pl.BlockSpec((OUT_TB, D_MODEL), lambda i: (i, 0)),
            pl.BlockSpec((OUT_TB, D_MODEL), lambda i: (i, 0)),
            pl.BlockSpec((D_MODEL, D_MODEL), lambda i: (0, 0), pipeline_mode=pl.Buffered(1)),
            pl.BlockSpec((1, D_MODEL), lambda i: (0, 0)),
            pl.BlockSpec((N_EXPERTS, D_MODEL), lambda i: (0, 0)),
        ],
        out_specs=[
            pl.BlockSpec((OUT_TB, D_MODEL), lambda i: (i, 0)),
            pl.BlockSpec((OUT_TB, D_MODEL), lambda i: (i, 0)),
            pl.BlockSpec((N_EXPERTS, OUT_TB), lambda i: (0, i)),
        ],
        out_shape=[jax.ShapeDtypeStruct((n, D_MODEL), F32),
                   jax.ShapeDtypeStruct((n, D_MODEL), F32),
                   jax.ShapeDtypeStruct((N_EXPERTS, n), F32)],
        compiler_params=_cparams(("arbitrary",)),
        name="out_proj",
    )(merged, x2d, w_out, g_ffn, w_rt)


F32_INF_BITS = 0x7F800000
VALUE_BISECT_STEPS = 31


def _route_thr_kernel(aff_ref, thr_ref, cut_ref, *, cap, ne):
    nb = aff_ref.shape[0] // ne
    bits = pltpu.bitcast(aff_ref[...], jnp.int32)
    tok = (lax.broadcasted_iota(jnp.int32, (nb, LANES), 0) * LANES
           + lax.broadcasted_iota(jnp.int32, (nb, LANES), 1))
    seg_r = lax.broadcasted_iota(jnp.int32, (ne, ne * nb), 0)
    seg_c = lax.broadcasted_iota(jnp.int32, (ne, ne * nb), 1)
    seg = jnp.where((seg_c >= seg_r * nb) & (seg_c < (seg_r + 1) * nb), 1.0, 0.0).astype(BF16)
    ones = jnp.ones((LANES, LANES), BF16)

    def count(masks):
        m = jnp.concatenate([jnp.where(mk, 1.0, 0.0) for mk in masks], axis=0).astype(BF16)
        rows = jnp.dot(m, ones, preferred_element_type=F32)
        return jnp.dot(seg, rows.astype(BF16), preferred_element_type=F32)

    def per_expert(fn):
        return [fn(bits[e * nb:(e + 1) * nb], e) for e in range(ne)]

    def vstep(_, lohi):
        lo, hi = lohi
        mid = lo + ((hi - lo + 1) >> 1)
        ok = count(per_expert(lambda b, e: b >= mid[e:e + 1])) >= cap
        return jnp.where(ok, mid, lo), jnp.where(ok, hi, mid - 1)

    thr, _ = lax.fori_loop(0, VALUE_BISECT_STEPS, vstep,
                           (jnp.zeros((ne, LANES), jnp.int32), jnp.full((ne, LANES), F32_INF_BITS, jnp.int32)))
    need = cap - count(per_expert(lambda b, e: b > thr[e:e + 1]))

    def tstep(_, lohi):
        lo, hi = lohi
        mid = lo + ((hi - lo) >> 1)
        ok = count(per_expert(lambda b, e: (b == thr[e:e + 1]) & (tok <= mid[e:e + 1]))) >= need
        return jnp.where(ok, lo, mid + 1), jnp.where(ok, mid, hi)

    n_tok_bits = max(1, (nb * LANES - 1).bit_length())
    cut, _ = lax.fori_loop(0, n_tok_bits, tstep,
                           (jnp.zeros((ne, LANES), jnp.int32), jnp.full((ne, LANES), nb * LANES - 1, jnp.int32)))
    thr_ref[...] = thr
    cut_ref[...] = cut


def _route_compact_kernel(aff_ref, thr_ref, cut_ref, idx_ref, gate_ref, *, cap):
    aff = aff_ref[0]
    nb = aff.shape[0]
    bits = pltpu.bitcast(aff, jnp.int32)
    tok = (lax.broadcasted_iota(jnp.int32, (nb, LANES), 0) * LANES
           + lax.broadcasted_iota(jnp.int32, (nb, LANES), 1))
    thr, cut = thr_ref[0], cut_ref[0]
    sel = jnp.where((bits > thr) | ((bits == thr) & (tok <= cut)), 1.0, 0.0).astype(BF16)

    r_i = lax.broadcasted_iota(jnp.int32, (LANES, LANES), 0)
    c_i = lax.broadcasted_iota(jnp.int32, (LANES, LANES), 1)
    upper = jnp.where(r_i <= c_i, 1.0, 0.0).astype(BF16)
    cw = jnp.dot(sel, upper, preferred_element_type=F32)
    cnt_row = lax.dot_general(jnp.ones((8, LANES), BF16), sel, (((1,), (1,)), ((), ())),
                              preferred_element_type=F32)
    rb = lax.broadcasted_iota(jnp.int32, (nb, nb), 0)
    cb = lax.broadcasted_iota(jnp.int32, (nb, nb), 1)
    upper_nb = jnp.where(rb <= cb, 1.0, 0.0).astype(BF16)
    incl_row = jnp.dot(cnt_row.astype(BF16), upper_nb, preferred_element_type=F32)[0:1]
    excl_row = incl_row - cnt_row[0:1]

    slot = lax.broadcasted_iota(jnp.int32, (cap, 1), 0).astype(F32)
    blk = jnp.sum(jnp.where(incl_row <= slot, 1.0, 0.0), axis=1, keepdims=True)
    onehot_b = lax.broadcasted_iota(jnp.int32, (cap, nb), 1).astype(F32) == blk
    base = jnp.sum(jnp.where(onehot_b, excl_row, 0.0), axis=1, keepdims=True)
    oh = jnp.where(onehot_b, 1.0, 0.0).astype(BF16)
    cw_j = jnp.dot(oh, cw.astype(BF16), preferred_element_type=F32)
    loc = jnp.sum(jnp.where(cw_j <= slot - base, 1.0, 0.0), axis=1, keepdims=True)
    idx = (blk * LANES + loc).astype(jnp.int32)
    idx_ref[0] = jnp.broadcast_to(idx, (cap, LANES))

    a_hi = aff.astype(BF16)
    r1 = aff - a_hi.astype(F32)
    a_mid = r1.astype(BF16)
    a_lo = (r1 - a_mid.astype(F32)).astype(BF16)
    aff_j = (jnp.dot(oh, a_hi, preferred_element_type=F32) + jnp.dot(oh, a_mid, preferred_element_type=F32)
             + jnp.dot(oh, a_lo, preferred_element_type=F32))
    lane = lax.broadcasted_iota(jnp.int32, (cap, LANES), 1).astype(F32)
    gate = jnp.sum(jnp.where(lane == loc, aff_j, 0.0), axis=1, keepdims=True)
    gate_ref[0] = jnp.broadcast_to(gate, (cap, LANES))


def _route(aff3, cap):
    e, nb, _ = aff3.shape
    thr, cut = pl.pallas_call(
        functools.partial(_route_thr_kernel, cap=cap, ne=e),
        grid=(1,),
        in_specs=[pl.BlockSpec((e * nb, LANES), lambda i: (0, 0))],
        out_specs=[pl.BlockSpec((e, LANES), lambda i: (0, 0))] * 2,
        out_shape=[jax.ShapeDtypeStruct((e, LANES), jnp.int32)] * 2,
        compiler_params=_cparams(("arbitrary",)),
        name="route_thr",
    )(aff3.reshape(e * nb, LANES))
    row = pl.BlockSpec((1, 1, LANES), lambda i: (i, 0, 0))
    return pl.pallas_call(
        functools.partial(_route_compact_kernel, cap=cap),
        grid=(e,),
        in_specs=[pl.BlockSpec((1, nb, LANES), lambda i: (i, 0, 0)), row, row],
        out_specs=[pl.BlockSpec((1, cap, LANES), lambda i: (i, 0, 0)),
                   pl.BlockSpec((1, cap, LANES), lambda i: (i, 0, 0))],
        out_shape=[jax.ShapeDtypeStruct((e, cap, LANES), jnp.int32),
                   jax.ShapeDtypeStruct((e, cap, LANES), F32)],
        compiler_params=_cparams(("arbitrary",)),
        name="route_compact",
    )(aff3, thr.reshape(e, 1, LANES), cut.reshape(e, 1, LANES))


FFN_TM = 256


def _ffn_kernel(idxp_ref, idxc_ref, idxn_ref, gate_ref, xn_hbm, acc_in_hbm, wg_hbm, wu_hbm, wd_hbm, acc_hbm,
                w_scr, xbuf, abuf, obuf, xbf, sem_w, sem_x, sem_a, sem_o, *, tm, nt, ne):
    del acc_in_hbm
    e, r = pl.program_id(0), pl.program_id(1)
    s = e * nt + r
    slot = s % 2
    other = 1 - slot
    last_step = ne * nt - 1

    def x_row(idx_ref, j, sl):
        return pltpu.make_async_copy(xn_hbm.at[pl.ds(idx_ref[0, 0, 0, j], 1), :],
                                     xbuf.at[sl, pl.ds(j, 1), :], sem_x.at[sl])

    def a_row(idx_ref, j, sl):
        return pltpu.make_async_copy(acc_hbm.at[pl.ds(idx_ref[0, 0, 0, j], 1), :],
                                     abuf.at[sl, pl.ds(j, 1), :], sem_a.at[sl])

    def o_row(idx_ref, j, sl):
        return pltpu.make_async_copy(obuf.at[sl, pl.ds(j, 1), :],
                                     acc_hbm.at[pl.ds(idx_ref[0, 0, 0, j], 1), :], sem_o.at[sl])

    def wait_tile(buf, sem, sl):
        pltpu.make_async_copy(buf.at[sl], buf.at[sl], sem.at[sl]).wait()

    @pl.when(r == 0)
    def _():
        copies = [pltpu.make_async_copy(w.at[e], w_scr.at[k], sem_w.at[k])
                  for k, w in enumerate((wg_hbm, wu_hbm, wd_hbm))]
        for cp in copies:
            cp.start()
        for cp in copies:
            cp.wait()

    @pl.when(s == 0)
    def _():
        def first(j, c_):
            x_row(idxc_ref, j, 0).start()
            a_row(idxc_ref, j, 0).start()
            return c_
        lax.fori_loop(0, tm, first, 0)

    wait_tile(xbuf, sem_x, slot)

    @pl.when((r >= 2) | ((r == 0) & (s >= 2)))
    def _():
        wait_tile(obuf, sem_o, slot)

    xbf[...] = xbuf[slot].astype(BF16)

    def compute(with_acc):
        x = xbf[...]
        hid = (jax.nn.silu(jnp.dot(x, w_scr[0], preferred_element_type=F32))
               * jnp.dot(x, w_scr[1], preferred_element_type=F32))
        y = jnp.dot(hid.astype(BF16), w_scr[2], preferred_element_type=F32) * gate_ref[0][:, 0:1]
        obuf[slot] = (abuf[slot] + y) if with_acc else y

    def start_gather_next(with_acc):
        for j in range(tm):
            x_row(idxn_ref, j, other).start()
            if with_acc:
                a_row(idxn_ref, j, other).start()

    @pl.when(r == 0)
    def _():
        @pl.when(s > 0)
        def _():
            def writeback_prev(j, c_):
                o_row(idxp_ref, j, other).start()
                return c_
            lax.fori_loop(0, tm, writeback_prev, 0)
        start_gather_next(True)
        compute(False)

        @pl.when(s > 0)
        def _():
            wait_tile(obuf, sem_o, other)
            def gather(j, c_):
                a_row(idxc_ref, j, slot).start()
                return c_
            lax.fori_loop(0, tm, gather, 0)
        wait_tile(abuf, sem_a, slot)
        obuf[slot] = obuf[slot] + abuf[slot]

    @pl.when(r > 0)
    def _():
        wait_tile(abuf, sem_a, slot)

    @pl.when((r > 0) & (r < nt - 1))
    def _():
        for j in range(tm):
            o_row(idxp_ref, j, other).start()
        start_gather_next(True)
        compute(True)

    @pl.when(r == nt - 1)
    def _():
        for j in range(tm):
            o_row(idxp_ref, j, other).start()
        start_gather_next(False)
        compute(True)

    @pl.when(s == last_step)
    def _():
        def writeback(j, c_):
            o_row(idxc_ref, j, slot).start()
            return c_
        lax.fori_loop(0, tm, writeback, 0)
        wait_tile(obuf, sem_o, slot)
        wait_tile(obuf, sem_o, other)
        wait_tile(xbuf, sem_x, other)


def _ffn(idx4, gate, xn, acc, w_gate, w_up, w_down):
    e, nt, _, tm = idx4.shape
    assert nt >= 2
    anyspec = pl.BlockSpec(memory_space=pl.ANY)

    def prev(i, r):
        s = jnp.maximum(i * nt + r - 1, 0)
        return (s // nt, s % nt, 0, 0)

    def nxt(i, r):
        s = jnp.minimum(i * nt + r + 1, e * nt - 1)
        return (s // nt, s % nt, 0, 0)

    idx_spec = lambda f: pl.BlockSpec((1, 1, 1, tm), f, memory_space=pltpu.SMEM)
    return pl.pallas_call(
        functools.partial(_ffn_kernel, tm=tm, nt=nt, ne=e),
        grid=(e, nt),
        in_specs=[
            idx_spec(prev), idx_spec(lambda i, r: (i, r, 0, 0)), idx_spec(nxt),
            pl.BlockSpec((1, tm, LANES), lambda i, r: (i, r, 0)),
            anyspec, anyspec, anyspec, anyspec, anyspec,
        ],
        out_specs=anyspec,
        out_shape=jax.ShapeDtypeStruct(acc.shape, F32),
        scratch_shapes=[pltpu.VMEM((3, D_MODEL, D_MODEL), BF16),
                        pltpu.VMEM((2, tm, D_MODEL), F32),
                        pltpu.VMEM((2, tm, D_MODEL), F32),
                        pltpu.VMEM((2, tm, D_MODEL), F32),
                        pltpu.VMEM((tm, D_MODEL), BF16),
                        pltpu.SemaphoreType.DMA((3,)),
                        pltpu.SemaphoreType.DMA((2,)),
                        pltpu.SemaphoreType.DMA((2,)),
                        pltpu.SemaphoreType.DMA((2,))],
        input_output_aliases={5: 0},
        compiler_params=_cparams(("arbitrary", "arbitrary")),
        name="ffn",
    )(idx4, idx4, idx4, gate, xn, acc, w_gate, w_up, w_down)


def _prep_weights(g_mix, w_in, g_q, g_k, attn_sink, lam_re, lam_im, log_step, b_re, b_im, c_re, c_im,
                  d_skip, w_glu, w_br_attn, w_br_ssm, w_out, g_ffn, w_router, w_gate, w_up, w_down):
    u0 = D_ATTN + 2 * D_KV
    w_main = jnp.concatenate([w_in[:, :D_ATTN], w_in[:, u0 + D_SSM:], w_in[:, D_ATTN:u0]], axis=1).astype(BF16)
    w_ut = w_in[:, u0:u0 + D_SSM].T.astype(BF16)
    g_ = N_SSM_GROUPS
    ls = jnp.broadcast_to(log_step[:, :, None], (2, g_, P))
    zeros = jnp.zeros((g_, P), F32)
    prow = jnp.stack([lam_re[0], lam_im[0], ls[0], lam_re[1], lam_im[1], ls[1], zeros, zeros], axis=1)
    pcol = jnp.swapaxes(prow, 1, 2)
    tr = lambda a: jnp.swapaxes(a, 1, 2)
    bt = jnp.concatenate([tr(b_re[0]), tr(b_im[0]), tr(b_re[1]), tr(b_im[1])], axis=1)
    c = jnp.concatenate([c_re[0], c_im[0], c_re[1], c_im[1]], axis=1)
    ct = jnp.swapaxes(c, 1, 2)
    dvec = jnp.repeat(d_skip.reshape(g_, H), CHUNK, axis=1).reshape(g_, 1, D_FLAT)
    return dict(
        g_mix=g_mix.reshape(1, D_MODEL), w_main=w_main, w_ut=w_ut,
        g_q=g_q.reshape(1, HEAD_DIM), g_k=g_k.reshape(1, HEAD_DIM), sink=attn_sink,
        prow=prow, pcol=pcol, bt=bt, c=c, ct=ct, dvec=dvec,
        w_glu=w_glu.astype(BF16), w_a=w_br_attn.astype(BF16), w_s=w_br_ssm.astype(BF16),
        w_out=w_out.astype(BF16), g_ffn=g_ffn.reshape(1, D_MODEL), w_rt=w_router.T.astype(BF16),
        w_gate=w_gate.astype(BF16), w_up=w_up.astype(BF16), w_down=w_down.astype(BF16))


def _layers(xs, w):
    shapes = [x.shape for x in xs]
    x2ds = [x.reshape(b * s, D_MODEL) for x, (b, s, _) in zip(xs, shapes)]
    pre = [_in_proj(x2d, w["g_mix"], w["g_q"], w["g_k"], w["w_main"], w["w_ut"]) for x2d in x2ds]
    u4s = [ut.reshape(N_SSM_GROUPS, H, ut.shape[1] // CHUNK, CHUNK) for _, ut in pre]
    g4s = _ssm(u4s, w["prow"], w["pcol"], w["bt"], w["c"], w["ct"], w["dvec"], [s for _, s, _ in shapes])
    outs = []
    for (b, s, _), x2d, (proj, _), g4 in zip(shapes, x2ds, pre, g4s):
        n = b * s
        attn = _attention(proj, w["sink"], s)
        g = g4.reshape(D_SSM, n)
        merged = _mix(attn, g, proj, w["w_glu"], w["w_a"], w["w_s"])
        x2, xn, afft = _out_proj(merged, x2d, w["w_out"], w["g_ffn"], w["w_rt"])
        cap = max(1, min(n, EC_CAPACITY_FACTOR * n // N_EXPERTS))
        idx, gate = _route(afft.reshape(N_EXPERTS, n // LANES, LANES), cap)
        tm = min(FFN_TM, cap // 2)
        idx4 = idx[:, :, 0].reshape(N_EXPERTS, cap // tm, 1, tm)
        y = _ffn(idx4, gate, xn, x2, w["w_gate"], w["w_up"], w["w_down"])
        outs.append(y.reshape(b, s, D_MODEL))
    return outs


def kernel(x_prompt, x_sample, g_mix, w_in, g_q, g_k, attn_sink, lam_re, lam_im, log_step, b_re, b_im,
           c_re, c_im, d_skip, w_glu, w_br_attn, w_br_ssm, w_out, g_ffn, w_router, w_gate, w_up, w_down):
    depth = g_mix.shape[0]
    y_prompt, y_sample = x_prompt, x_sample
    for l in range(depth):
        w = _prep_weights(g_mix[l], w_in[l], g_q[l], g_k[l], attn_sink[l], lam_re[l], lam_im[l],
                          log_step[l], b_re[l], b_im[l], c_re[l], c_im[l], d_skip[l], w_glu[l],
                          w_br_attn[l], w_br_ssm[l], w_out[l], g_ffn[l], w_router[l], w_gate[l],
                          w_up[l], w_down[l])
        y_prompt, y_sample = _layers([y_prompt, y_sample], w)
    return (y_prompt, y_sample)
```
